```python
import math
import jax
import jax.numpy as jnp
from jax import lax
import numpy as np

D_MODEL = 2048
BATCH = 4
SEQ = 2048
DEPTH = 4

N_MIXERS = 2
N_A_LAYERS = (DEPTH + 1) // 2
N_B_LAYERS = DEPTH // 2
EPS = 1e-6
NEG = -1e30

A_HEAD_DIM = 128
A_HEADS_PER_GROUP = 4
A_PATTERNS = ((128, 1), (512, 4), (2048, 16))
A_N_GROUPS = len(A_PATTERNS)
A_HEADS = A_HEADS_PER_GROUP * A_N_GROUPS
A_QKV_W = A_HEADS * A_HEAD_DIM
A_OUT_W = A_HEADS_PER_GROUP * A_HEAD_DIM
QBLK = 128

CHUNK = 128
B_GROUPS = 12
B_GROUP_W = 128
B_W = B_GROUPS * B_GROUP_W

MEM_LEN = 256
MEM_HEADS = 4
MEM_HEAD_DIM = 128
MEM_W = MEM_HEADS * MEM_HEAD_DIM

A_IN = 3 * A_QKV_W + MEM_W
A_OUT_IN = A_OUT_W + MEM_W
B_IN = 2 * B_W + MEM_W
B_OUT_IN = B_W + MEM_W

FF = 5632
CONV_W = 3

kernel_name = "hybrid_dilated_sgu_memory_encoder"


def rmsnorm(x, g):
    xf = x.astype(jnp.float32)
    y = xf * lax.rsqrt(jnp.mean(xf * xf, axis=-1, keepdims=True) + EPS)
    return (y * g.astype(jnp.float32)).astype(x.dtype)


def alibi_slopes():
    return (2.0 ** (-8.0 * (np.arange(A_HEADS) + 1) / A_HEADS)).astype(np.float32)


def dilated_window_attention(q, k, v, dilation, n_side, slopes):
    B, S, H, E = q.shape
    L = S // dilation
    nblk = -(-L // QBLK)
    Lp = nblk * QBLK
    W = QBLK + 2 * n_side
    qs = q.reshape(B, L, dilation, H, E)
    ks = k.reshape(B, L, dilation, H, E)
    vs = v.reshape(B, L, dilation, H, E)
    qb = jnp.pad(qs, ((0, 0), (0, Lp - L), (0, 0), (0, 0), (0, 0)))
    qb = qb.reshape(B, nblk, QBLK, dilation, H, E)
    pad_k = ((0, 0), (n_side, n_side + Lp - L), (0, 0), (0, 0), (0, 0))
    kp = jnp.pad(ks, pad_k)
    vp = jnp.pad(vs, pad_k)
    idx = np.arange(nblk)[:, None] * QBLK + np.arange(W)[None, :]
    kb = kp[:, idx]
    vb = vp[:, idx]
    s = jnp.einsum('bnqrhe,bnkrhe->bnrhqk', qb.astype(jnp.float32),
                   kb.astype(jnp.float32)) * (E ** -0.5)
    rel = np.arange(W)[None, :] - n_side - np.arange(QBLK)[:, None]
    band = np.abs(rel) <= n_side
    jk = np.arange(nblk)[:, None] * QBLK - n_side + np.arange(W)[None, :]
    valid = (jk >= 0) & (jk < L)
    mask = band[None, :, :] & valid[:, None, :]
    dist = (np.abs(rel) * dilation).astype(np.float32)
    alibi = -slopes[:, None, None] * dist[None]
    s = s + alibi[None, None, None]
    s = jnp.where(mask[None, :, None, None], s, NEG)
    lse = jax.nn.logsumexp(s, axis=-1)
    p = jnp.exp(s - lse[..., None])
    o = jnp.einsum('bnrhqk,bnkrhe->bnqrhe', p, vb.astype(jnp.float32))
    o = o.reshape(B, Lp, dilation, H, E)[:, :L].reshape(B, S, H, E)
    lse = lse.transpose(0, 1, 4, 2, 3).reshape(B, Lp, dilation, H)[:, :L].reshape(B, S, H)
    return o, lse


def mixer_a(proj):
    B, S, _ = proj.shape
    qkv = proj.reshape(B, S, 3, A_N_GROUPS, A_HEADS_PER_GROUP, A_HEAD_DIM)
    slopes_all = jnp.asarray(alibi_slopes())
    outs, lses = [], []
    for g, (window, dilation) in enumerate(A_PATTERNS):
        n_side = (window // 2) // dilation
        sl = slopes_all[g * A_HEADS_PER_GROUP:(g + 1) * A_HEADS_PER_GROUP]
        o, l = dilated_window_attention(qkv[:, :, 0, g], qkv[:, :, 1, g],
                                        qkv[:, :, 2, g], dilation, n_side, sl)
        outs.append(o)
        lses.append(l)
    outs = jnp.stack(outs)
    wts = jax.nn.softmax(jnp.stack(lses), axis=0)
    comb = jnp.sum(wts[..., None] * outs, axis=0)
    return comb.reshape(B, S, A_OUT_W).astype(proj.dtype)


def mixer_b(proj_uv, v_norm_g, w_s, s_bias):
    B, S, _ = proj_uv.shape
    uv = jax.nn.gelu(proj_uv, approximate=False)
    u, v = uv[..., :B_W], uv[..., B_W:]
    v = rmsnorm(v, v_norm_g)
    vc = v.reshape(B, S // CHUNK, CHUNK, B_GROUPS, B_GROUP_W)
    mixed = jnp.einsum('gpq,bcqge->bcpge', w_s, vc) + s_bias.T[None, None, :, :, None]
    return u * mixed.reshape(B, S, B_W)


def memory_cross_attention(qm, mem_n, w_kv):
    B, S, _ = qm.shape
    M = mem_n.shape[1]
    kv = jnp.einsum('bmd,df->bmf', mem_n, w_kv).reshape(B, M, 2, MEM_HEADS, MEM_HEAD_DIM)
    q = qm.reshape(B, S, MEM_HEADS, MEM_HEAD_DIM).astype(jnp.float32)
    k = kv[:, :, 0].astype(jnp.float32)
    v = kv[:, :, 1].astype(jnp.float32)
    s = jnp.einsum('bshe,bmhe->bhsm', q, k) * (MEM_HEAD_DIM ** -0.5)
    p = jax.nn.softmax(s, axis=-1)
    o = jnp.einsum('bhsm,bmhe->bshe', p, v)
    return o.reshape(B, S, MEM_W).astype(qm.dtype)


def conv_ffn(h, w_up, conv_w, conv_b, w_down):
    a = jnp.einsum('bsd,df->bsf', h, w_up)
    ap = jnp.pad(a, ((0, 0), (1, 1), (0, 0)))
    a = ap[:, :-2] * conv_w[0] + ap[:, 1:-1] * conv_w[1] + ap[:, 2:] * conv_w[2] + conv_b
    gate, val = a[..., :FF], a[..., FF:]
    return jnp.einsum('bsf,fd->bsd', jax.nn.gelu(gate, approximate=False) * val, w_down)


def setup_inputs(seed: int = 0) -> dict:
    key = jax.random.key(seed)
    ks = jax.random.split(key, 20)

    def nrm(k, shape, scale):
        return jax.random.normal(k, shape, jnp.float32) * scale

    D = D_MODEL
    return {
        "x": nrm(ks[0], (BATCH, SEQ, D), 1.0),
        "mem": nrm(ks[1], (BATCH, MEM_LEN, D), 1.0),
        "mix_norm_g": 1.0 + nrm(ks[2], (DEPTH, D), 0.02),
        "ffn_norm_g": 1.0 + nrm(ks[3], (DEPTH, D), 0.02),
        "mem_norm_g": 1.0 + nrm(ks[4], (DEPTH, D), 0.02),
        "w_mem_kv": nrm(ks[5], (DEPTH, D, 2 * MEM_W), D ** -0.5),
        "a_w_in": nrm(ks[6], (N_A_LAYERS, D, A_IN), D ** -0.5),
        "a_w_out": nrm(ks[7], (N_A_LAYERS, A_OUT_IN, D), A_OUT_IN ** -0.5),
        "b_w_in": nrm(ks[8], (N_B_LAYERS, D, B_IN), D ** -0.5),
        "b_v_norm_g": 1.0 + nrm(ks[9], (N_B_LAYERS, B_W), 0.02),
        "b_w_s": nrm(ks[10], (N_B_LAYERS, B_GROUPS, CHUNK, CHUNK), CHUNK ** -0.5),
        "b_s_bias": 1.0 + nrm(ks[11], (N_B_LAYERS, B_GROUPS, CHUNK), 0.02),
        "b_w_out": nrm(ks[12], (N_B_LAYERS, B_OUT_IN, D), B_OUT_IN ** -0.5),
        "ffn_w_up": nrm(ks[13], (DEPTH, D, 2 * FF), D ** -0.5),
        "ffn_conv_w": nrm(ks[14], (DEPTH, CONV_W, 2 * FF), CONV_W ** -0.5),
        "ffn_conv_b": nrm(ks[15], (DEPTH, 2 * FF), 0.02),
        "ffn_w_down": nrm(ks[16], (DEPTH, FF, D), FF ** -0.5),
        "final_norm_g": 1.0 + nrm(ks[17], (D,), 0.02),
    }


def reference(x, mem, mix_norm_g, ffn_norm_g, mem_norm_g, w_mem_kv, a_w_in, a_w_out,
              b_w_in, b_v_norm_g, b_w_s, b_s_bias, b_w_out, ffn_w_up, ffn_conv_w,
              ffn_conv_b, ffn_w_down, final_norm_g):
    for i in range(DEPTH):
        h = rmsnorm(x, mix_norm_g[i])
        mem_n = rmsnorm(mem, mem_norm_g[i])
        j = i // N_MIXERS
        if i % N_MIXERS == 0:
            proj = jnp.einsum('bsd,df->bsf', h, a_w_in[j])
            tok = mixer_a(proj[..., :3 * A_QKV_W])
            mem_out = memory_cross_attention(proj[..., 3 * A_QKV_W:], mem_n, w_mem_kv[i])
            w_out = a_w_out[j]
        else:
            proj = jnp.einsum('bsd,df->bsf', h, b_w_in[j])
            tok = mixer_b(proj[..., :2 * B_W], b_v_norm_g[j], b_w_s[j], b_s_bias[j])
            mem_out = memory_cross_attention(proj[..., 2 * B_W:], mem_n, w_mem_kv[i])
            w_out = b_w_out[j]
        cat = jnp.concatenate([tok, mem_out], axis=-1)
        x = x + jnp.einsum('bsf,fd->bsd', cat, w_out)
        h = rmsnorm(x, ffn_norm_g[i])
        x = x + conv_ffn(h, ffn_w_up[i], ffn_conv_w[i], ffn_conv_b[i], ffn_w_down[i])
    return rmsnorm(x, final_norm_g)
```

```python
import functools
import math

import numpy as np
import jax
import jax.numpy as jnp
from jax import lax
from jax.experimental import pallas as pl
from jax.experimental.pallas import tpu as pltpu

D_MODEL = 2048
SEQ = 2048
DEPTH = 4
EPS = 1e-6
NEG = -1e30

HEAD_DIM = 128
HEADS_PER_GROUP = 4
A_PATTERNS = ((128, 1), (512, 4), (2048, 16))
A_HEADS = HEADS_PER_GROUP * len(A_PATTERNS)
A_QKV_W = A_HEADS * HEAD_DIM
GROUP_W = HEADS_PER_GROUP * HEAD_DIM
QBLK = 128

CHUNK = 128
B_GROUPS = 12
B_W = B_GROUPS * 128

MEM_LEN = 256
MEM_HEADS = 4
MEM_W = MEM_HEADS * HEAD_DIM

A_IN = 3 * A_QKV_W + MEM_W
B_IN = 2 * B_W + MEM_W
FF = 5632

VMEM_LIMIT_BYTES = 58 * 1024 * 1024

_SQRT_HALF = 0.7071067811865476


def _params(n_axes):
    return pltpu.CompilerParams(
        dimension_semantics=("arbitrary",) * n_axes,
        vmem_limit_bytes=VMEM_LIMIT_BYTES,
    )


def _rms(x, g):
    y = x * lax.rsqrt(jnp.mean(x * x, axis=-1, keepdims=True) + EPS)
    return y * g


def _gelu(x):
    return 0.5 * x * (1.0 + lax.erf(x * _SQRT_HALF))


def _alibi_slopes():
    return (2.0 ** (-8.0 * (np.arange(A_HEADS) + 1) / A_HEADS)).astype(np.float32)


def _rms_matmul_kernel(x_ref, g_ref, w_ref, o_ref, h_ref, *, gelu):
    @pl.when(pl.program_id(1) == 0)
    def _():
        h_ref[...] = _rms(x_ref[...], g_ref[...]).astype(jnp.bfloat16)

    acc = jnp.dot(h_ref[...], w_ref[...], preferred_element_type=jnp.float32)
    if gelu:
        acc = _gelu(acc)
    o_ref[...] = acc.astype(o_ref.dtype)


def _rms_matmul(x, g, w, *, tm, tn, out_dtype, gelu=False, name):
    m, d = x.shape
    n = w.shape[1]
    assert m % tm == 0 and n % tn == 0
    return pl.pallas_call(
        functools.partial(_rms_matmul_kernel, gelu=gelu),
        grid=(m // tm, n // tn),
        in_specs=[
            pl.BlockSpec((tm, d), lambda i, j: (i, 0)),
            pl.BlockSpec((1, d), lambda i, j: (0, 0)),
            pl.BlockSpec((d, tn), lambda i, j: (0, j)),
        ],
        out_specs=pl.BlockSpec((tm, tn), lambda i, j: (i, j)),
        out_shape=jax.ShapeDtypeStruct((m, n), out_dtype),
        scratch_shapes=[pltpu.VMEM((tm, d), jnp.bfloat16)],
        compiler_params=_params(2),
        name=name,
    )(x, g, w)


def _band_attn_kernel(q_ref, k_ref, v_ref, o_ref, lse_ref, *, length, n_side, dilation, slopes):
    kw = min(length, QBLK + 2 * n_side)
    nblk = length // QBLK
    scale = HEAD_DIM ** -0.5

    def block(qi, carry):
        q0 = pl.multiple_of(qi * QBLK, QBLK)
        k0 = pl.multiple_of(jnp.clip(qi * QBLK - n_side, 0, length - kw), 64)
        qpos = q0 + lax.broadcasted_iota(jnp.int32, (QBLK, kw), 0)
        kpos = k0 + lax.broadcasted_iota(jnp.int32, (QBLK, kw), 1)
        dist = jnp.abs(kpos - qpos)
        band = dist <= n_side
        fdist = (dist * dilation).astype(jnp.float32)
        for h in range(HEADS_PER_GROUP):
            cols = slice(h * HEAD_DIM, (h + 1) * HEAD_DIM)
            q = q_ref[0, pl.ds(q0, QBLK), cols]
            k = k_ref[0, pl.ds(k0, kw), cols]
            v = v_ref[0, pl.ds(k0, kw), cols]
            s = lax.dot_general(q, k, (((1,), (1,)), ((), ())),
                                preferred_element_type=jnp.float32) * scale
            s = s + (-float(slopes[h])) * fdist
            s = jnp.where(band, s, NEG)
            m = jnp.max(s, axis=-1, keepdims=True)
            p = jnp.exp(s - m)
            l = jnp.sum(p, axis=-1, keepdims=True)
            o = jnp.dot(p.astype(jnp.bfloat16), v, preferred_element_type=jnp.float32)
            o_ref[0, pl.ds(q0, QBLK), cols] = o / l
            lse_ref[0, pl.ds(q0, QBLK), cols] = jnp.broadcast_to(m + jnp.log(l), (QBLK, HEAD_DIM))
        return carry

    lax.fori_loop(0, nblk, block, 0)


def _band_attention(proj, group, *, batch):
    window, dilation = A_PATTERNS[group]
    n_side = (window // 2) // dilation
    length = SEQ // dilation
    slopes = _alibi_slopes()[group * HEADS_PER_GROUP:(group + 1) * HEADS_PER_GROUP]
    pv = proj.reshape(batch, length, dilation * A_IN)
    per_row = A_IN // GROUP_W

    def col(which):
        base = which * (A_QKV_W // GROUP_W) + group
        return lambda b, r: (b, 0, r * per_row + base)

    out_sds = jax.ShapeDtypeStruct((batch, length, dilation * GROUP_W), jnp.float32)
    o, lse = pl.pallas_call(
        functools.partial(_band_attn_kernel, length=length, n_side=n_side,
                          dilation=dilation, slopes=tuple(float(s) for s in slopes)),
        grid=(batch, dilation),
        in_specs=[pl.BlockSpec((1, length, GROUP_W), col(w)) for w in range(3)],
        out_specs=[pl.BlockSpec((1, length, GROUP_W), lambda b, r: (b, 0, r))] * 2,
        out_shape=[out_sds, out_sds],
        compiler_params=_params(2),
        name=f"band_attn_g{group}",
    )(pv, pv, pv)
    return o.reshape(batch * SEQ, GROUP_W), lse.reshape(batch * SEQ, GROUP_W)


def _mem_attention(qm, kv_ref):
    scale = HEAD_DIM ** -0.5
    outs = []
    for h in range(MEM_HEADS):
        q = qm[:, h * HEAD_DIM:(h + 1) * HEAD_DIM]
        k = kv_ref[0, :, h * HEAD_DIM:(h + 1) * HEAD_DIM]
        v = kv_ref[0, :, MEM_W + h * HEAD_DIM:MEM_W + (h + 1) * HEAD_DIM]
        s = lax.dot_general(q, k, (((1,), (1,)), ((), ())),
                            preferred_element_type=jnp.float32) * scale
        m = jnp.max(s, axis=-1, keepdims=True)
        p = jnp.exp(s - m)
        l = jnp.sum(p, axis=-1, keepdims=True)
        o = jnp.dot(p.astype(jnp.bfloat16), v, preferred_element_type=jnp.float32)
        outs.append(o / l)
    return jnp.concatenate(outs, axis=-1)


def _mix_out_a_kernel(x_ref, o0_ref, o1_ref, o2_ref, l0_ref, l1_ref, l2_ref,
                      qm_ref, kv_ref, w_ref, out_ref):
    l0, l1, l2 = l0_ref[...], l1_ref[...], l2_ref[...]
    mx = jnp.maximum(jnp.maximum(l0, l1), l2)
    e0, e1, e2 = jnp.exp(l0 - mx), jnp.exp(l1 - mx), jnp.exp(l2 - mx)
    den = e0 + e1 + e2
    tok = (e0 / den) * o0_ref[...] + (e1 / den) * o1_ref[...] + (e2 / den) * o2_ref[...]
    mem_out = _mem_attention(qm_ref[...], kv_ref)
    cat = jnp.concatenate([tok, mem_out], axis=-1).astype(jnp.bfloat16)
    out_ref[...] = x_ref[...] + jnp.dot(cat, w_ref[...], preferred_element_type=jnp.float32)


def _mix_out_a(x, o, lse, proj, kv, w_out, *, tm):
    m, d = x.shape
    tiles_per_seq = SEQ // tm
    row = lambda i: (i, 0)
    grp = pl.BlockSpec((tm, GROUP_W), row)
    return pl.pallas_call(
        _mix_out_a_kernel,
        grid=(m // tm,),
        in_specs=[
            pl.BlockSpec((tm, d), row),
            grp, grp, grp, grp, grp, grp,
            pl.BlockSpec((tm, MEM_W), lambda i: (i, 3 * A_QKV_W // MEM_W)),
            pl.BlockSpec((1, MEM_LEN, 2 * MEM_W), lambda i: (i // tiles_per_seq, 0, 0)),
            pl.BlockSpec(w_out.shape, lambda i: (0, 0)),
        ],
        out_specs=pl.BlockSpec((tm, d), row),
        out_shape=jax.ShapeDtypeStruct((m, d), jnp.float32),
        compiler_params=_params(1),
        name="mix_out_a",
    )(x, o[0], o[1], o[2], lse[0], lse[1], lse[2], proj, kv, w_out)


def _mix_out_b_kernel(x_ref, u_ref, v_ref, vg_ref, ws_ref, sb_ref, qm_ref, kv_ref, w_ref,
                      out_ref, tok_ref):
    tm = x_ref.shape[0]
    vn = _rms(v_ref[...], vg_ref[...]).astype(jnp.bfloat16)
    for c in range(tm // CHUNK):
        rows = slice(c * CHUNK, (c + 1) * CHUNK)
        for g in range(B_GROUPS):
            cols = slice(g * 128, (g + 1) * 128)
            mixed = jnp.dot(ws_ref[g], vn[rows, cols], preferred_element_type=jnp.float32)
            mixed = mixed + sb_ref[:, g:g + 1]
            tok_ref[rows, cols] = (u_ref[rows, cols] * mixed).astype(jnp.bfloat16)
    mem_out = _mem_attention(qm_ref[...], kv_ref).astype(jnp.bfloat16)
    cat = jnp.concatenate([tok_ref[...], mem_out], axis=-1)
    out_ref[...] = x_ref[...] + jnp.dot(cat, w_ref[...], preferred_element_type=jnp.float32)


def _mix_out_b(x, uv, qm_src, kv, v_norm_g, w_s, s_bias_t, w_out, *, tm):
    m, d = x.shape
    tiles_per_seq = SEQ // tm
    row = lambda i: (i, 0)
    return pl.pallas_call(
        _mix_out_b_kernel,
        grid=(m // tm,),
        in_specs=[
            pl.BlockSpec((tm, d), row),
            pl.BlockSpec((tm, B_W), row),
            pl.BlockSpec((tm, B_W), lambda i: (i, 1)),
            pl.BlockSpec((1, B_W), lambda i: (0, 0)),
            pl.BlockSpec(w_s.shape, lambda i: (0, 0, 0)),
            pl.BlockSpec(s_bias_t.shape, lambda i: (0, 0)),
            pl.BlockSpec((tm, MEM_W), row),
            pl.BlockSpec((1, MEM_LEN, 2 * MEM_W), lambda i: (i // tiles_per_seq, 0, 0)),
            pl.BlockSpec(w_out.shape, lambda i: (0, 0)),
        ],
        out_specs=pl.BlockSpec((tm, d), row),
        out_shape=jax.ShapeDtypeStruct((m, d), jnp.float32),
        scratch_shapes=[pltpu.VMEM((tm, B_W), jnp.bfloat16)],
        compiler_params=_params(1),
        name="mix_out_b",
    )(x, uv, uv, v_norm_g, w_s, s_bias_t, qm_src, kv, w_out)


HALO = 16


def _conv_ffn_kernel(x_ref, xp_ref, xn_ref, g_ref, wg_ref, wv_ref, cw_ref, cb_ref, wd_ref,
                     fg_ref, out_ref, h_ref, ag_ref, av_ref, *, tiles_per_seq, final_norm):
    i = pl.program_id(0)
    j = pl.program_id(1)
    nj = pl.num_programs(1)
    tm = x_ref.shape[0]
    tf = wg_ref.shape[1]

    @pl.when(j == 0)
    def _():
        g = g_ref[...]
        h_ref[0:tm, :] = _rms(x_ref[...], g).astype(jnp.bfloat16)
        first = (i % tiles_per_seq) == 0
        last = (i % tiles_per_seq) == tiles_per_seq - 1
        r = lax.broadcasted_iota(jnp.int32, (HALO, 1), 0)
        take_next = jnp.logical_and(r == 0, jnp.logical_not(last))
        take_prev = jnp.logical_and(r == HALO - 1, jnp.logical_not(first))
        halo = jnp.where(take_next, _rms(xn_ref[...], g),
                         jnp.where(take_prev, _rms(xp_ref[...], g), 0.0))
        h_ref[tm:tm + HALO, :] = halo.astype(jnp.bfloat16)
        out_ref[...] = x_ref[...]

    h = h_ref[...]

    def conv(w_ref, a_ref, half):
        a = jnp.dot(h, w_ref[...], preferred_element_type=jnp.float32)
        a_ref[HALO:HALO + tm, :] = a[0:tm]
        a_ref[0:HALO, :] = a[tm:tm + HALO]
        a_ref[HALO + tm:2 * HALO + tm, :] = a[tm:tm + HALO]
        cw = cw_ref[half]
        return (a_ref[HALO - 1:HALO - 1 + tm, :] * cw[0:1]
                + a_ref[HALO:HALO + tm, :] * cw[1:2]
                + a_ref[HALO + 1:HALO + 1 + tm, :] * cw[2:3]
                + cb_ref[half])

    gate = conv(wg_ref, ag_ref, 0)
    val = conv(wv_ref, av_ref, 1)
    act = (_gelu(gate) * val).astype(jnp.bfloat16)
    out_ref[...] += jnp.dot(act, wd_ref[...], preferred_element_type=jnp.float32)

    if final_norm:
        @pl.when(j == nj - 1)
        def _():
            out_ref[...] = _rms(out_ref[...], fg_ref[...])


def _conv_ffn(x, g, w_up, conv_w, conv_b, w_down, final_g, *, tm, tf, final_norm):
    m, d = x.shape
    nf = FF // tf
    assert m % tm == 0 and FF % tf == 0 and SEQ % tm == 0 and tm % HALO == 0
    tiles_per_seq = SEQ // tm
    hb = tm // HALO
    n_hblocks = m // HALO
    cw = conv_w.reshape(3, 2, FF).transpose(1, 0, 2)
    cb = conv_b.reshape(2, 1, FF)
    return pl.pallas_call(
        functools.partial(_conv_ffn_kernel, tiles_per_seq=tiles_per_seq, final_norm=final_norm),
        grid=(m // tm, nf),
        in_specs=[
            pl.BlockSpec((tm, d), lambda i, j: (i, 0)),
            pl.BlockSpec((HALO, d), lambda i, j: (jnp.maximum(i * hb - 1, 0), 0)),
            pl.BlockSpec((HALO, d), lambda i, j: (jnp.minimum((i + 1) * hb, n_hblocks - 1), 0)),
            pl.BlockSpec((1, d), lambda i, j: (0, 0)),
            pl.BlockSpec((d, tf), lambda i, j: (0, j)),
            pl.BlockSpec((d, tf), lambda i, j: (0, nf + j)),
            pl.BlockSpec((2, 3, tf), lambda i, j: (0, 0, j)),
            pl.BlockSpec((2, 1, tf), lambda i, j: (0, 0, j)),
            pl.BlockSpec((tf, d), lambda i, j: (j, 0)),
            pl.BlockSpec((1, d), lambda i, j: (0, 0)),
        ],
        out_specs=pl.BlockSpec((tm, d), lambda i, j: (i, 0)),
        out_shape=jax.ShapeDtypeStruct((m, d), jnp.float32),
        scratch_shapes=[
            pltpu.VMEM((tm + HALO, d), jnp.bfloat16),
            pltpu.VMEM((tm + 2 * HALO, tf), jnp.float32),
            pltpu.VMEM((tm + 2 * HALO, tf), jnp.float32),
        ],
        compiler_params=_params(2),
        name="conv_ffn",
    )(x, x, x, g, w_up, w_up, cw, cb, w_down, final_g)


def kernel(x, mem, mix_norm_g, ffn_norm_g, mem_norm_g, w_mem_kv, a_w_in, a_w_out, b_w_in,
           b_v_norm_g, b_w_s, b_s_bias, b_w_out, ffn_w_up, ffn_conv_w, ffn_conv_b, ffn_w_down,
           final_norm_g):
    batch, seq, d = x.shape
    assert (seq, d) == (SEQ, D_MODEL)
    bf = jnp.bfloat16
    xs = x.reshape(batch * seq, d)
    mems = mem.reshape(batch * MEM_LEN, d)
    final_g = final_norm_g.reshape(1, d)

    for i in range(DEPTH):
        j = i // 2
        kv = _rms_matmul(mems, mem_norm_g[i].reshape(1, d), w_mem_kv[i].astype(bf),
                         tm=256, tn=512, out_dtype=bf, name="mem_kv")
        kv = kv.reshape(batch, MEM_LEN, 2 * MEM_W)
        mix_g = mix_norm_g[i].reshape(1, d)
        if i % 2 == 0:
            proj = _rms_matmul(xs, mix_g, a_w_in[j].astype(bf), tm=512, tn=512,
                               out_dtype=bf, name="in_proj_a")
            o, lse = zip(*[_band_attention(proj, grp, batch=batch) for grp in range(len(A_PATTERNS))])
            xs = _mix_out_a(xs, o, lse, proj, kv, a_w_out[j].astype(bf), tm=512)
        else:
            w_in = b_w_in[j].astype(bf)
            uv = _rms_matmul(xs, mix_g, w_in[:, :2 * B_W], tm=512, tn=512, out_dtype=jnp.float32,
                             gelu=True, name="in_proj_b_uv")
            qm = _rms_matmul(xs, mix_g, w_in[:, 2 * B_W:], tm=512, tn=512, out_dtype=bf,
                             name="in_proj_b_qm")
            xs = _mix_out_b(xs, uv, qm, kv, b_v_norm_g[j].reshape(1, B_W), b_w_s[j].astype(bf),
                            b_s_bias[j].T, b_w_out[j].astype(bf), tm=512)
        xs = _conv_ffn(xs, ffn_norm_g[i].reshape(1, d), ffn_w_up[i].astype(bf), ffn_conv_w[i],
                       ffn_conv_b[i], ffn_w_down[i].astype(bf), final_g,
                       tm=512, tf=512, final_norm=(i == DEPTH - 1))
    return xs.reshape(batch, seq, d)
```

```python
import functools

import numpy as np
import jax
import jax.numpy as jnp
from jax import lax
from jax.experimental import pallas as pl
from jax.experimental.pallas import tpu as pltpu

D_MODEL = 2048
SEQ = 2048
DEPTH = 4
EPS = 1e-6
NEG = -1e30

HEAD_DIM = 128
HEADS_PER_GROUP = 4
A_PATTERNS = ((128, 1), (512, 4), (2048, 16))
A_GROUPS = len(A_PATTERNS)
A_HEADS = HEADS_PER_GROUP * A_GROUPS
A_QKV_W = A_HEADS * HEAD_DIM
GROUP_W = HEADS_PER_GROUP * HEAD_DIM
QBLK = 128

CHUNK = 128
B_GROUPS = 12
B_W = B_GROUPS * 128

MEM_LEN = 256
MEM_HEADS = 4
MEM_W = MEM_HEADS * HEAD_DIM

A_IN = 3 * A_QKV_W + MEM_W
B_IN = 2 * B_W + MEM_W
FF = 5632

VMEM_LIMIT_BYTES = 58 * 1024 * 1024

_SQRT_HALF = 0.7071067811865476


def _params(n_axes):
    return pltpu.CompilerParams(
        dimension_semantics=("arbitrary",) * n_axes,
        vmem_limit_bytes=VMEM_LIMIT_BYTES,
    )


def _rms(x, g):
    y = x * lax.rsqrt(jnp.mean(x * x, axis=-1, keepdims=True) + EPS)
    return y * g


def _gelu(x):
    return 0.5 * x * (1.0 + lax.erf(x * _SQRT_HALF))


def _alibi_slopes():
    return (2.0 ** (-8.0 * (np.arange(A_HEADS) + 1) / A_HEADS)).astype(np.float32)


def _layer_spec(block, index_map, layer):
    return pl.BlockSpec((None,) + block, lambda *g: (layer,) + index_map(*g))


def _rms_matmul_kernel(x_ref, g_ref, w_ref, o_ref, h_ref, *, gelu_cols):
    j = pl.program_id(1)
    tn = o_ref.shape[1]

    @pl.when(j == 0)
    def _():
        h_ref[...] = _rms(x_ref[...], g_ref[...]).astype(jnp.bfloat16)

    acc = jnp.dot(h_ref[...], w_ref[...], preferred_element_type=jnp.float32)
    full, part = divmod(gelu_cols, tn)

    if gelu_cols == 0:
        o_ref[...] = acc.astype(o_ref.dtype)
        return

    @pl.when(j < full)
    def _():
        o_ref[...] = _gelu(acc).astype(o_ref.dtype)

    @pl.when(j == full)
    def _():
        if part:
            o_ref[:, :part] = _gelu(acc[:, :part]).astype(o_ref.dtype)
        o_ref[:, part:] = acc[:, part:].astype(o_ref.dtype)

    @pl.when(j > full)
    def _():
        o_ref[...] = acc.astype(o_ref.dtype)


def _rms_matmul(x, g_all, w_all, layer_g, layer_w, *, tm, tn, out_dtype, gelu_cols=0, name):
    m, d = x.shape
    n = w_all.shape[2]
    assert m % tm == 0 and n % tn == 0
    return pl.pallas_call(
        functools.partial(_rms_matmul_kernel, gelu_cols=gelu_cols),
        grid=(m // tm, n // tn),
        in_specs=[
            pl.BlockSpec((tm, d), lambda i, j: (i, 0)),
            _layer_spec((1, d), lambda i, j: (0, 0), layer_g),
            _layer_spec((d, tn), lambda i, j: (0, j), layer_w),
        ],
        out_specs=pl.BlockSpec((tm, tn), lambda i, j: (i, j)),
        out_shape=jax.ShapeDtypeStruct((m, n), out_dtype),
        scratch_shapes=[pltpu.VMEM((tm, d), jnp.bfloat16)],
        compiler_params=_params(2),
        name=name,
    )(x, g_all, w_all)


def _band_block(qs_ref, ks_ref, vs_ref, qi, *, length, n_side, dilation, slope):
    kw = min(length, QBLK + 2 * n_side)
    scale = HEAD_DIM ** -0.5
    q0 = pl.multiple_of(qi * QBLK, QBLK)
    k0 = pl.multiple_of(jnp.clip(qi * QBLK - n_side, 0, length - kw), 64)
    qpos = q0 + lax.broadcasted_iota(jnp.int32, (QBLK, kw), 0)
    kpos = k0 + lax.broadcasted_iota(jnp.int32, (QBLK, kw), 1)
    dist = jnp.abs(kpos - qpos)
    q = qs_ref[pl.ds(q0, QBLK), :]
    k = ks_ref[pl.ds(k0, kw), :]
    v = vs_ref[pl.ds(k0, kw), :]
    s = lax.dot_general(q, k, (((1,), (1,)), ((), ())),
                        preferred_element_type=jnp.float32) * scale
    s = s + (-slope) * (dist * dilation).astype(jnp.float32)
    s = jnp.where(dist <= n_side, s, NEG)
    m = jnp.max(s, axis=-1, keepdims=True)
    p = jnp.exp(s - m)
    l = jnp.sum(p, axis=-1, keepdims=True)
    o = jnp.dot(p.astype(jnp.bfloat16), v, preferred_element_type=jnp.float32)
    return o / l, m + jnp.log(l)


def _dilated_attn_kernel(*refs, slopes):
    qkv_refs = refs[:3 * A_GROUPS]
    tok_ref = refs[3 * A_GROUPS]
    o_scr, lse_scr, qs_ref, ks_ref, vs_ref = refs[3 * A_GROUPS + 1:]
    head = pl.program_id(1)

    for grp, (window, dilation) in enumerate(A_PATTERNS):
        n_side = (window // 2) // dilation
        length = SEQ // dilation
        nblk = length // QBLK
        q_ref, k_ref, v_ref = qkv_refs[3 * grp:3 * grp + 3]
        slope = slopes[grp, head]

        def residue(r, carry, grp=grp, dilation=dilation, n_side=n_side, length=length,
                    nblk=nblk, q_ref=q_ref, k_ref=k_ref, v_ref=v_ref, slope=slope):
            rows = pl.ds(r, length, stride=dilation) if dilation > 1 else pl.ds(0, length)
            qs_ref[0:length, :] = q_ref[rows, :].astype(jnp.bfloat16)
            ks_ref[0:length, :] = k_ref[rows, :].astype(jnp.bfloat16)
            vs_ref[0:length, :] = v_ref[rows, :].astype(jnp.bfloat16)

            def block(qi, c):
                o, lse = _band_block(qs_ref, ks_ref, vs_ref, qi, length=length, n_side=n_side,
                                     dilation=dilation, slope=slope)
                start = qi * (QBLK * dilation) + r
                dst = pl.ds(start, QBLK, stride=dilation) if dilation > 1 else pl.ds(start, QBLK)
                o_scr[grp, dst, :] = o
                lse_scr[grp, dst, :] = jnp.broadcast_to(lse, (QBLK, HEAD_DIM))
                return c

            return lax.fori_loop(0, nblk, block, carry)

        lax.fori_loop(0, dilation, residue, 0)

    rows_per_step = 256
    for c in range(SEQ // rows_per_step):
        rows = slice(c * rows_per_step, (c + 1) * rows_per_step)
        lses = [lse_scr[grp, rows, :] for grp in range(A_GROUPS)]
        mx = functools.reduce(jnp.maximum, lses)
        es = [jnp.exp(l - mx) for l in lses]
        den = functools.reduce(lambda a, b: a + b, es)
        tok = sum((e / den) * o_scr[grp, rows, :] for grp, e in enumerate(es))
        tok_ref[rows, :] = tok.astype(tok_ref.dtype)


def _dilated_attention(proj, *, batch):
    slopes = jnp.asarray(_alibi_slopes().reshape(A_GROUPS, HEADS_PER_GROUP))

    def col(which, grp):
        base = which * A_HEADS + grp * HEADS_PER_GROUP
        return lambda b, h, sl: (b, base + h)

    in_specs = [pl.BlockSpec((SEQ, HEAD_DIM), col(which, grp))
                for grp in range(A_GROUPS) for which in range(3)]

    def body(slopes_ref, *refs):
        _dilated_attn_kernel(*refs, slopes=slopes_ref)

    return pl.pallas_call(
        body,
        grid_spec=pltpu.PrefetchScalarGridSpec(
            num_scalar_prefetch=1,
            grid=(batch, HEADS_PER_GROUP),
            in_specs=in_specs,
            out_specs=pl.BlockSpec((SEQ, HEAD_DIM), lambda b, h, sl: (b, h)),
            scratch_shapes=[
                pltpu.VMEM((A_GROUPS, SEQ, HEAD_DIM), jnp.float32),
                pltpu.VMEM((A_GROUPS, SEQ, HEAD_DIM), jnp.float32),
                pltpu.VMEM((SEQ, HEAD_DIM), jnp.bfloat16),
                pltpu.VMEM((SEQ, HEAD_DIM), jnp.bfloat16),
                pltpu.VMEM((SEQ, HEAD_DIM), jnp.bfloat16),
            ],
        ),
        out_shape=jax.ShapeDtypeStruct((batch * SEQ, GROUP_W), jnp.bfloat16),
        compiler_params=_params(2),
        name="dilated_attn",
    )(slopes, *([proj] * (3 * A_GROUPS)))


def _mem_attention(qm, kv_ref):
    scale = HEAD_DIM ** -0.5
    outs = []
    for h in range(MEM_HEADS):
        q = qm[:, h * HEAD_DIM:(h + 1) * HEAD_DIM]
        k = kv_ref[:, h * HEAD_DIM:(h + 1) * HEAD_DIM]
        v = kv_ref[:, MEM_W + h * HEAD_DIM:MEM_W + (h + 1) * HEAD_DIM]
        s = lax.dot_general(q, k, (((1,), (1,)), ((), ())),
                            preferred_element_type=jnp.float32) * scale
        m = jnp.max(s, axis=-1, keepdims=True)
        p = jnp.exp(s - m)
        l = jnp.sum(p, axis=-1, keepdims=True)
        o = jnp.dot(p.astype(jnp.bfloat16), v, preferred_element_type=jnp.float32)
        outs.append(o / l)
    return jnp.concatenate(outs, axis=-1)


def _mix_out_a_kernel(x_ref, tok_ref, qm_ref, kv_ref, w_ref, out_ref):
    mem_out = _mem_attention(qm_ref[...].astype(jnp.bfloat16), kv_ref).astype(jnp.bfloat16)
    cat = jnp.concatenate([tok_ref[...], mem_out], axis=-1)
    out_ref[...] = x_ref[...] + jnp.dot(cat, w_ref[...], preferred_element_type=jnp.float32)


def _mix_out_a(x, tok, proj, kv, w_out_all, layer, *, tm):
    m, d = x.shape
    tiles_per_seq = SEQ // tm
    row = lambda i: (i, 0)
    return pl.pallas_call(
        _mix_out_a_kernel,
        grid=(m // tm,),
        in_specs=[
            pl.BlockSpec((tm, d), row),
            pl.BlockSpec((tm, GROUP_W), row),
            pl.BlockSpec((tm, MEM_W), lambda i: (i, 3 * A_QKV_W // MEM_W)),
            pl.BlockSpec((MEM_LEN, 2 * MEM_W), lambda i: (i // tiles_per_seq, 0)),
            _layer_spec(w_out_all.shape[1:], lambda i: (0, 0), layer),
        ],
        out_specs=pl.BlockSpec((tm, d), row),
        out_shape=jax.ShapeDtypeStruct((m, d), jnp.float32),
        compiler_params=_params(1),
        name="mix_out_a",
    )(x, tok, proj, kv, w_out_all)


def _mix_out_b_kernel(x_ref, u_ref, v_ref, qm_ref, vg_ref, ws_ref, sb_ref, kv_ref, w_ref,
                      out_ref, tok_ref):
    tm = x_ref.shape[0]
    vn = _rms(v_ref[...], vg_ref[...]).astype(jnp.bfloat16)
    for c in range(tm // CHUNK):
        rows = slice(c * CHUNK, (c + 1) * CHUNK)
        for g in range(B_GROUPS):
            cols = slice(g * 128, (g + 1) * 128)
            mixed = jnp.dot(ws_ref[g], vn[rows, cols], preferred_element_type=jnp.float32)
            mixed = mixed + sb_ref[:, g:g + 1]
            tok_ref[rows, cols] = (u_ref[rows, cols] * mixed).astype(jnp.bfloat16)
    mem_out = _mem_attention(qm_ref[...].astype(jnp.bfloat16), kv_ref).astype(jnp.bfloat16)
    cat = jnp.concatenate([tok_ref[...], mem_out], axis=-1)
    out_ref[...] = x_ref[...] + jnp.dot(cat, w_ref[...], preferred_element_type=jnp.float32)


def _mix_out_b(x, uvq, kv, v_norm_g_all, w_s_all, s_bias_t_all, w_out_all, layer, *, tm):
    m, d = x.shape
    tiles_per_seq = SEQ // tm
    row = lambda i: (i, 0)
    const2 = lambda i: (0, 0)
    return pl.pallas_call(
        _mix_out_b_kernel,
        grid=(m // tm,),
        in_specs=[
            pl.BlockSpec((tm, d), row),
            pl.BlockSpec((tm, B_W), row),
            pl.BlockSpec((tm, B_W), lambda i: (i, 1)),
            pl.BlockSpec((tm, MEM_W), lambda i: (i, 2 * B_W // MEM_W)),
            _layer_spec((1, B_W), const2, layer),
            _layer_spec(w_s_all.shape[1:], lambda i: (0, 0, 0), layer),
            _layer_spec(s_bias_t_all.shape[1:], const2, layer),
            pl.BlockSpec((MEM_LEN, 2 * MEM_W), lambda i: (i // tiles_per_seq, 0)),
            _layer_spec(w_out_all.shape[1:], const2, layer),
        ],
        out_specs=pl.BlockSpec((tm, d), row),
        out_shape=jax.ShapeDtypeStruct((m, d), jnp.float32),
        scratch_shapes=[pltpu.VMEM((tm, B_W), jnp.bfloat16)],
        compiler_params=_params(1),
        name="mix_out_b",
    )(x, uvq, uvq, uvq, v_norm_g_all, w_s_all, s_bias_t_all, kv, w_out_all)


HALO = 16


def _conv_ffn_kernel(x_ref, xp_ref, xn_ref, g_ref, wg_ref, wv_ref, cw_ref, cb_ref, wd_ref,
                     fg_ref, out_ref, h_ref, ag_ref, av_ref, *, tiles_per_seq, final_norm):
    i = pl.program_id(0)
    j = pl.program_id(1)
    nj = pl.num_programs(1)
    tm = x_ref.shape[0]

    @pl.when(j == 0)
    def _():
        g = g_ref[...]
        h_ref[0:tm, :] = _rms(x_ref[...], g).astype(jnp.bfloat16)
        first = (i % tiles_per_seq) == 0
        last = (i % tiles_per_seq) == tiles_per_seq - 1
        r = lax.broadcasted_iota(jnp.int32, (HALO, 1), 0)
        take_next = jnp.logical_and(r == 0, jnp.logical_not(last))
        take_prev = jnp.logical_and(r == HALO - 1, jnp.logical_not(first))
        halo = jnp.where(take_next, _rms(xn_ref[...], g),
                         jnp.where(take_prev, _rms(xp_ref[...], g), 0.0))
        h_ref[tm:tm + HALO, :] = halo.astype(jnp.bfloat16)
        out_ref[...] = x_ref[...]

    h = h_ref[...]

    def conv(w_ref, a_ref, half):
        a = jnp.dot(h, w_ref[...], preferred_element_type=jnp.float32)
        a_ref[HALO:HALO + tm, :] = a[0:tm]
        a_ref[0:HALO, :] = a[tm:tm + HALO]
        a_ref[HALO + tm:2 * HALO + tm, :] = a[tm:tm + HALO]
        cw = cw_ref[half]
        return (a_ref[HALO - 1:HALO - 1 + tm, :] * cw[0:1]
                + a_ref[HALO:HALO + tm, :] * cw[1:2]
                + a_ref[HALO + 1:HALO + 1 + tm, :] * cw[2:3]
                + cb_ref[half])

    gate = conv(wg_ref, ag_ref, 0)
    val = conv(wv_ref, av_ref, 1)
    act = (_gelu(gate) * val).astype(jnp.bfloat16)
    out_ref[...] += jnp.dot(act, wd_ref[...], preferred_element_type=jnp.float32)

    if final_norm:
        @pl.when(j == nj - 1)
        def _():
            out_ref[...] = _rms(out_ref[...], fg_ref[...])


def _conv_ffn(x, g_all, w_up_all, cw_all, cb_all, w_down_all, final_g, layer, *, tm, tf,
              final_norm):
    m, d = x.shape
    nf = FF // tf
    assert m % tm == 0 and FF % tf == 0 and SEQ % tm == 0 and tm % HALO == 0
    tiles_per_seq = SEQ // tm
    hb = tm // HALO
    n_hblocks = m // HALO
    return pl.pallas_call(
        functools.partial(_conv_ffn_kernel, tiles_per_seq=tiles_per_seq, final_norm=final_norm),
        grid=(m // tm, nf),
        in_specs=[
            pl.BlockSpec((tm, d), lambda i, j: (i, 0)),
            pl.BlockSpec((HALO, d), lambda i, j: (jnp.maximum(i * hb - 1, 0), 0)),
            pl.BlockSpec((HALO, d), lambda i, j: (jnp.minimum((i + 1) * hb, n_hblocks - 1), 0)),
            _layer_spec((1, d), lambda i, j: (0, 0), layer),
            _layer_spec((d, tf), lambda i, j: (0, j), layer),
            _layer_spec((d, tf), lambda i, j: (0, nf + j), layer),
            _layer_spec((2, 3, tf), lambda i, j: (0, 0, j), layer),
            _layer_spec((2, 1, tf), lambda i, j: (0, 0, j), layer),
            _layer_spec((tf, d), lambda i, j: (j, 0), layer),
            pl.BlockSpec((1, d), lambda i, j: (0, 0)),
        ],
        out_specs=pl.BlockSpec((tm, d), lambda i, j: (i, 0)),
        out_shape=jax.ShapeDtypeStruct((m, d), jnp.float32),
        scratch_shapes=[
            pltpu.VMEM((tm + HALO, d), jnp.bfloat16),
            pltpu.VMEM((tm + 2 * HALO, tf), jnp.float32),
            pltpu.VMEM((tm + 2 * HALO, tf), jnp.float32),
        ],
        compiler_params=_params(2),
        name="conv_ffn",
    )(x, x, x, g_all, w_up_all, w_up_all, cw_all, cb_all, w_down_all, final_g)


def kernel(x, mem, mix_norm_g, ffn_norm_g, mem_norm_g, w_mem_kv, a_w_in, a_w_out, b_w_in,
           b_v_norm_g, b_w_s, b_s_bias, b_w_out, ffn_w_up, ffn_conv_w, ffn_conv_b, ffn_w_down,
           final_norm_g):
    batch, seq, d = x.shape
    assert (seq, d) == (SEQ, D_MODEL)
    bf = jnp.bfloat16
    xs = x.reshape(batch * seq, d)
    mems = mem.reshape(batch * MEM_LEN, d)

    mix_g = mix_norm_g.reshape(DEPTH, 1, d)
    ffn_g = ffn_norm_g.reshape(DEPTH, 1, d)
    mem_g = mem_norm_g.reshape(DEPTH, 1, d)
    final_g = final_norm_g.reshape(1, d)
    v_norm_g = b_v_norm_g.reshape(-1, 1, B_W)
    s_bias_t = jnp.swapaxes(b_s_bias, 1, 2)
    conv_w = ffn_conv_w.reshape(DEPTH, 3, 2, FF).transpose(0, 2, 1, 3)
    conv_b = ffn_conv_b.reshape(DEPTH, 2, 1, FF)
    w_mem_kv, a_w_in, a_w_out, b_w_in, b_w_s, b_w_out, ffn_w_up, ffn_w_down = (
        w.astype(bf) for w in (w_mem_kv, a_w_in, a_w_out, b_w_in, b_w_s, b_w_out, ffn_w_up,
                               ffn_w_down))

    for i in range(DEPTH):
        j = i // 2
        kv = _rms_matmul(mems, mem_g, w_mem_kv, i, i, tm=512, tn=1024, out_dtype=bf,
                         name="mem_kv")
        if i % 2 == 0:
            proj = _rms_matmul(xs, mix_g, a_w_in, i, j, tm=1024, tn=1280,
                               out_dtype=jnp.float32, name="in_proj_a")
            tok = _dilated_attention(proj, batch=batch)
            xs = _mix_out_a(xs, tok, proj, kv, a_w_out, j, tm=512)
        else:
            uvq = _rms_matmul(xs, mix_g, b_w_in, i, j, tm=1024, tn=1792,
                              out_dtype=jnp.float32, gelu_cols=2 * B_W, name="in_proj_b")
            xs = _mix_out_b(xs, uvq, kv, v_norm_g, b_w_s, s_bias_t, b_w_out, j, tm=512)
        xs = _conv_ffn(xs, ffn_g, ffn_w_up, conv_w, conv_b, ffn_w_down, final_g, i,
                       tm=512, tf=512, final_norm=(i == DEPTH - 1))
    return xs.reshape(batch, seq, d)
```

```python
import functools

import numpy as np
import jax
import jax.numpy as jnp
from jax import lax
from jax.experimental import pallas as pl
from jax.experimental.pallas import tpu as pltpu

D_MODEL = 2048
SEQ = 2048
DEPTH = 4
EPS = 1e-6
NEG = -1e30

HEAD_DIM = 128
HEADS_PER_GROUP = 4
A_PATTERNS = ((128, 1), (512, 4), (2048, 16))
A_GROUPS = len(A_PATTERNS)
A_HEADS = HEADS_PER_GROUP * A_GROUPS
A_QKV_W = A_HEADS * HEAD_DIM
GROUP_W = HEADS_PER_GROUP * HEAD_DIM
QBLK = 128

CHUNK = 128
B_GROUPS = 12
B_W = B_GROUPS * 128

MEM_LEN = 256
MEM_HEADS = 4
MEM_W = MEM_HEADS * HEAD_DIM

A_IN = 3 * A_QKV_W + MEM_W
B_IN = 2 * B_W + MEM_W
FF = 5632

VMEM_LIMIT_BYTES = 58 * 1024 * 1024

_SQRT_HALF = 0.7071067811865476


def _params(n_axes):
    return pltpu.CompilerParams(
        dimension_semantics=("arbitrary",) * n_axes,
        vmem_limit_bytes=VMEM_LIMIT_BYTES,
    )


def _rms(x, g):
    y = x * lax.rsqrt(jnp.mean(x * x, axis=-1, keepdims=True) + EPS)
    return y * g


def _gelu(x):
    return 0.5 * x * (1.0 + lax.erf(x * _SQRT_HALF))


def _alibi_slopes():
    return (2.0 ** (-8.0 * (np.arange(A_HEADS) + 1) / A_HEADS)).astype(np.float32)


def _layer_spec(block, index_map, layer):
    return pl.BlockSpec((None,) + block, lambda *g: (layer,) + index_map(*g))


def _rms_matmul_kernel(x_ref, g_ref, w_ref, o_ref, h_ref, *, gelu_cols):
    j = pl.program_id(1)
    tn = o_ref.shape[1]

    @pl.when(j == 0)
    def _():
        h_ref[...] = _rms(x_ref[...], g_ref[...]).astype(jnp.bfloat16)

    acc = jnp.dot(h_ref[...], w_ref[...], preferred_element_type=jnp.float32)
    full, part = divmod(gelu_cols, tn)

    if gelu_cols == 0:
        o_ref[...] = acc.astype(o_ref.dtype)
        return

    @pl.when(j < full)
    def _():
        o_ref[...] = _gelu(acc).astype(o_ref.dtype)

    @pl.when(j == full)
    def _():
        if part:
            o_ref[:, :part] = _gelu(acc[:, :part]).astype(o_ref.dtype)
        o_ref[:, part:] = acc[:, part:].astype(o_ref.dtype)

    @pl.when(j > full)
    def _():
        o_ref[...] = acc.astype(o_ref.dtype)


def _rms_matmul(x, g_all, w_all, layer_g, layer_w, *, tm, tn, out_dtype, gelu_cols=0, name):
    m, d = x.shape
    n = w_all.shape[2]
    assert m % tm == 0 and n % tn == 0
    return pl.pallas_call(
        functools.partial(_rms_matmul_kernel, gelu_cols=gelu_cols),
        grid=(m // tm, n // tn),
        in_specs=[
            pl.BlockSpec((tm, d), lambda i, j: (i, 0)),
            _layer_spec((1, d), lambda i, j: (0, 0), layer_g),
            _layer_spec((d, tn), lambda i, j: (0, j), layer_w),
        ],
        out_specs=pl.BlockSpec((tm, tn), lambda i, j: (i, j)),
        out_shape=jax.ShapeDtypeStruct((m, n), out_dtype),
        scratch_shapes=[pltpu.VMEM((tm, d), jnp.bfloat16)],
        compiler_params=_params(2),
        name=name,
    )(x, g_all, w_all)


def _group_geometry(grp):
    window, dilation = A_PATTERNS[grp]
    n_side = (window // 2) // dilation
    length = SEQ // dilation
    kw = min(length, QBLK + 2 * n_side)
    return dilation, n_side, length, kw


def _key_start(qi, n_side, length, kw):
    return min(max(qi * QBLK - n_side, 0), length - kw)


def _dilated_attn_kernel(slopes_ref, *refs):
    qkv_refs = refs[:3 * A_GROUPS]
    tok_ref = refs[3 * A_GROUPS]
    o_scr, lse_scr, bias_scr = refs[3 * A_GROUPS + 1:3 * A_GROUPS + 4]
    stage = refs[3 * A_GROUPS + 4:]
    head = pl.program_id(1)
    scale = HEAD_DIM ** -0.5

    for grp in range(A_GROUPS):
        dilation, n_side, length, kw = _group_geometry(grp)
        nblk = length // QBLK
        q_ref, k_ref, v_ref = qkv_refs[3 * grp:3 * grp + 3]
        qs_ref, ks_ref, vs_ref = stage[3 * grp:3 * grp + 3]
        slope = slopes_ref[grp, head]

        for r in range(dilation):
            rows = pl.ds(r, length, stride=dilation) if dilation > 1 else slice(None)
            dst = slice(r * length, (r + 1) * length)
            qs_ref[dst, :] = q_ref[rows, :].astype(jnp.bfloat16)
            ks_ref[dst, :] = k_ref[rows, :].astype(jnp.bfloat16)
            vs_ref[dst, 0:HEAD_DIM] = v_ref[rows, :].astype(jnp.bfloat16)
        vs_ref[:, HEAD_DIM:2 * HEAD_DIM] = jnp.ones((SEQ, HEAD_DIM), jnp.bfloat16)

        offsets = sorted({qi * QBLK - _key_start(qi, n_side, length, kw) for qi in range(nblk)})
        for t, off in enumerate(offsets):
            rel = (lax.broadcasted_iota(jnp.int32, (QBLK, kw), 1)
                   - lax.broadcasted_iota(jnp.int32, (QBLK, kw), 0) - off)
            dist = jnp.abs(rel)
            alibi = (-slope) * (dist * dilation).astype(jnp.float32)
            bias_scr[grp, t, :, 0:kw] = jnp.where(dist <= n_side, alibi, NEG)

        for r in range(dilation):
            for qi in range(nblk):
                k0 = _key_start(qi, n_side, length, kw)
                t = offsets.index(qi * QBLK - k0)
                q = qs_ref[r * length + qi * QBLK:r * length + (qi + 1) * QBLK, :]
                k = ks_ref[r * length + k0:r * length + k0 + kw, :]
                v1 = vs_ref[r * length + k0:r * length + k0 + kw, :]
                bias = bias_scr[grp, t, :, 0:kw]
                s = lax.dot_general(q, k, (((1,), (1,)), ((), ())),
                                    preferred_element_type=jnp.float32) * scale
                s = jnp.where(bias > 0.5 * NEG, s + bias, NEG)
                m = jnp.max(s, axis=-1, keepdims=True)
                p = jnp.exp(s - m).astype(jnp.bfloat16)
                ol = jnp.dot(p, v1, preferred_element_type=jnp.float32)
                l = ol[:, HEAD_DIM:]
                start = qi * QBLK * dilation + r
                dst = pl.ds(start, QBLK, stride=dilation) if dilation > 1 else pl.ds(start, QBLK)
                o_scr[grp, dst, :] = ol[:, :HEAD_DIM] / l
                lse_scr[grp, dst, :] = m + jnp.log(l)

    rows_per_step = 256
    for c in range(SEQ // rows_per_step):
        rows = slice(c * rows_per_step, (c + 1) * rows_per_step)
        lses = [lse_scr[grp, rows, :] for grp in range(A_GROUPS)]
        mx = functools.reduce(jnp.maximum, lses)
        es = [jnp.exp(l - mx) for l in lses]
        den = functools.reduce(lambda a, b: a + b, es)
        tok = sum((e / den) * o_scr[grp, rows, :] for grp, e in enumerate(es))
        tok_ref[rows, :] = tok.astype(tok_ref.dtype)


def _dilated_attention(proj, *, batch):
    slopes = jnp.asarray(_alibi_slopes().reshape(A_GROUPS, HEADS_PER_GROUP))

    def col(which, grp):
        base = which * A_HEADS + grp * HEADS_PER_GROUP
        return lambda b, h, sl: (b, base + h)

    in_specs = [pl.BlockSpec((SEQ, HEAD_DIM), col(which, grp))
                for grp in range(A_GROUPS) for which in range(3)]
    max_kw = max(_group_geometry(grp)[3] for grp in range(A_GROUPS))
    stage = []
    for _ in range(A_GROUPS):
        stage += [pltpu.VMEM((SEQ, HEAD_DIM), jnp.bfloat16),
                  pltpu.VMEM((SEQ, HEAD_DIM), jnp.bfloat16),
                  pltpu.VMEM((SEQ, 2 * HEAD_DIM), jnp.bfloat16)]
    return pl.pallas_call(
        _dilated_attn_kernel,
        grid_spec=pltpu.PrefetchScalarGridSpec(
            num_scalar_prefetch=1,
            grid=(batch, HEADS_PER_GROUP),
            in_specs=in_specs,
            out_specs=pl.BlockSpec((SEQ, HEAD_DIM), lambda b, h, sl: (b, h)),
            scratch_shapes=[
                pltpu.VMEM((A_GROUPS, SEQ, HEAD_DIM), jnp.float32),
                pltpu.VMEM((A_GROUPS, SEQ, HEAD_DIM), jnp.float32),
                pltpu.VMEM((A_GROUPS, 3, QBLK, max_kw), jnp.float32),
            ] + stage,
        ),
        out_shape=jax.ShapeDtypeStruct((batch * SEQ, GROUP_W), jnp.bfloat16),
        compiler_params=_params(2),
        name="dilated_attn",
    )(slopes, *([proj] * (3 * A_GROUPS)))


def _mem_attention(qm, kv_ref):
    scale = HEAD_DIM ** -0.5
    outs = []
    for h in range(MEM_HEADS):
        q = qm[:, h * HEAD_DIM:(h + 1) * HEAD_DIM]
        k = kv_ref[:, h * HEAD_DIM:(h + 1) * HEAD_DIM]
        v = kv_ref[:, MEM_W + h * HEAD_DIM:MEM_W + (h + 1) * HEAD_DIM]
        s = lax.dot_general(q, k, (((1,), (1,)), ((), ())),
                            preferred_element_type=jnp.float32) * scale
        m = jnp.max(s, axis=-1, keepdims=True)
        p = jnp.exp(s - m)
        l = jnp.sum(p, axis=-1, keepdims=True)
        o = jnp.dot(p.astype(jnp.bfloat16), v, preferred_element_type=jnp.float32)
        outs.append(o / l)
    return jnp.concatenate(outs, axis=-1)


def _mix_out_a_kernel(x_ref, tok_ref, qm_ref, kv_ref, w_ref, out_ref):
    mem_out = _mem_attention(qm_ref[...].astype(jnp.bfloat16), kv_ref).astype(jnp.bfloat16)
    cat = jnp.concatenate([tok_ref[...], mem_out], axis=-1)
    out_ref[...] = x_ref[...] + jnp.dot(cat, w_ref[...], preferred_element_type=jnp.float32)


def _mix_out_a(x, tok, proj, kv, w_out_all, layer, *, tm):
    m, d = x.shape
    tiles_per_seq = SEQ // tm
    row = lambda i: (i, 0)
    return pl.pallas_call(
        _mix_out_a_kernel,
        grid=(m // tm,),
        in_specs=[
            pl.BlockSpec((tm, d), row),
            pl.BlockSpec((tm, GROUP_W), row),
            pl.BlockSpec((tm, MEM_W), lambda i: (i, 3 * A_QKV_W // MEM_W)),
            pl.BlockSpec((MEM_LEN, 2 * MEM_W), lambda i: (i // tiles_per_seq, 0)),
            _layer_spec(w_out_all.shape[1:], lambda i: (0, 0), layer),
        ],
        out_specs=pl.BlockSpec((tm, d), row),
        out_shape=jax.ShapeDtypeStruct((m, d), jnp.float32),
        compiler_params=_params(1),
        name="mix_out_a",
    )(x, tok, proj, kv, w_out_all)


def _mix_out_b_kernel(x_ref, u_ref, v_ref, qm_ref, vg_ref, ws_ref, sb_ref, kv_ref, w_ref,
                      out_ref, tok_ref):
    tm = x_ref.shape[0]
    vn = _rms(v_ref[...], vg_ref[...]).astype(jnp.bfloat16)
    for c in range(tm // CHUNK):
        rows = slice(c * CHUNK, (c + 1) * CHUNK)
        for g in range(B_GROUPS):
            cols = slice(g * 128, (g + 1) * 128)
            mixed = jnp.dot(ws_ref[g], vn[rows, cols], preferred_element_type=jnp.float32)
            mixed = mixed + sb_ref[:, g:g + 1]
            tok_ref[rows, cols] = (u_ref[rows, cols] * mixed).astype(jnp.bfloat16)
    mem_out = _mem_attention(qm_ref[...].astype(jnp.bfloat16), kv_ref).astype(jnp.bfloat16)
    cat = jnp.concatenate([tok_ref[...], mem_out], axis=-1)
    out_ref[...] = x_ref[...] + jnp.dot(cat, w_ref[...], preferred_element_type=jnp.float32)


def _mix_out_b(x, uvq, kv, v_norm_g_all, w_s_all, s_bias_t_all, w_out_all, layer, *, tm):
    m, d = x.shape
    tiles_per_seq = SEQ // tm
    row = lambda i: (i, 0)
    const2 = lambda i: (0, 0)
    return pl.pallas_call(
        _mix_out_b_kernel,
        grid=(m // tm,),
        in_specs=[
            pl.BlockSpec((tm, d), row),
            pl.BlockSpec((tm, B_W), row),
            pl.BlockSpec((tm, B_W), lambda i: (i, 1)),
            pl.BlockSpec((tm, MEM_W), lambda i: (i, 2 * B_W // MEM_W)),
            _layer_spec((1, B_W), const2, layer),
            _layer_spec(w_s_all.shape[1:], lambda i: (0, 0, 0), layer),
            _layer_spec(s_bias_t_all.shape[1:], const2, layer),
            pl.BlockSpec((MEM_LEN, 2 * MEM_W), lambda i: (i // tiles_per_seq, 0)),
            _layer_spec(w_out_all.shape[1:], const2, layer),
        ],
        out_specs=pl.BlockSpec((tm, d), row),
        out_shape=jax.ShapeDtypeStruct((m, d), jnp.float32),
        scratch_shapes=[pltpu.VMEM((tm, B_W), jnp.bfloat16)],
        compiler_params=_params(1),
        name="mix_out_b",
    )(x, uvq, uvq, uvq, v_norm_g_all, w_s_all, s_bias_t_all, kv, w_out_all)


HALO = 16
FFN_TM = 512
FFN_TF = 512


def _conv_ffn_kernel(x_ref, xp_ref, xn_ref, g_ref, wg_ref, wv_ref, cw_ref, cb_ref, wd_ref,
                     fg_ref, out_ref, h_ref, ag_ref, av_ref, *, tiles_per_seq, final_norm):
    i = pl.program_id(0)
    j = pl.program_id(1)
    nj = pl.num_programs(1)
    tm = x_ref.shape[0]

    @pl.when(j == 0)
    def _():
        g = g_ref[...]
        h_ref[0:tm, :] = _rms(x_ref[...], g).astype(jnp.bfloat16)
        first = (i % tiles_per_seq) == 0
        last = (i % tiles_per_seq) == tiles_per_seq - 1
        r = lax.broadcasted_iota(jnp.int32, (HALO, 1), 0)
        take_next = jnp.logical_and(r == 0, jnp.logical_not(last))
        take_prev = jnp.logical_and(r == HALO - 1, jnp.logical_not(first))
        halo = jnp.where(take_next, _rms(xn_ref[...], g),
                         jnp.where(take_prev, _rms(xp_ref[...], g), 0.0))
        h_ref[tm:tm + HALO, :] = halo.astype(jnp.bfloat16)
        out_ref[...] = x_ref[...]

    h = h_ref[...]

    def conv(w_ref, a_ref, half):
        a = jnp.dot(h, w_ref[...], preferred_element_type=jnp.float32)
        a_ref[HALO:HALO + tm, :] = a[0:tm]
        a_ref[0:HALO, :] = a[tm:tm + HALO]
        a_ref[HALO + tm:2 * HALO + tm, :] = a[tm:tm + HALO]
        cw = cw_ref[half]
        return (a_ref[HALO - 1:HALO - 1 + tm, :] * cw[0:1]
                + a_ref[HALO:HALO + tm, :] * cw[1:2]
                + a_ref[HALO + 1:HALO + 1 + tm, :] * cw[2:3]
                + cb_ref[half])

    gate = conv(wg_ref, ag_ref, 0)
    val = conv(wv_ref, av_ref, 1)
    act = (_gelu(gate) * val).astype(jnp.bfloat16)
    out_ref[...] += jnp.dot(act, wd_ref[...], preferred_element_type=jnp.float32)

    if final_norm:
        @pl.when(j == nj - 1)
        def _():
            out_ref[...] = _rms(out_ref[...], fg_ref[...])


def _conv_ffn(x, g_all, w_up_all, cw_all, cb_all, w_down_all, final_g, layer, *, tm, tf,
              final_norm):
    m, d = x.shape
    nf = FF // tf
    assert m % tm == 0 and FF % tf == 0 and SEQ % tm == 0 and tm % HALO == 0
    assert w_up_all.shape[1:] == (2 * nf, d, tf)
    tiles_per_seq = SEQ // tm
    hb = tm // HALO
    n_hblocks = m // HALO
    return pl.pallas_call(
        functools.partial(_conv_ffn_kernel, tiles_per_seq=tiles_per_seq, final_norm=final_norm),
        grid=(m // tm, nf),
        in_specs=[
            pl.BlockSpec((tm, d), lambda i, j: (i, 0)),
            pl.BlockSpec((HALO, d), lambda i, j: (jnp.maximum(i * hb - 1, 0), 0)),
            pl.BlockSpec((HALO, d), lambda i, j: (jnp.minimum((i + 1) * hb, n_hblocks - 1), 0)),
            _layer_spec((1, d), lambda i, j: (0, 0), layer),
            pl.BlockSpec((None, None, d, tf), lambda i, j: (layer, j, 0, 0)),
            pl.BlockSpec((None, None, d, tf), lambda i, j: (layer, nf + j, 0, 0)),
            _layer_spec((2, 3, tf), lambda i, j: (0, 0, j), layer),
            _layer_spec((2, 1, tf), lambda i, j: (0, 0, j), layer),
            _layer_spec((tf, d), lambda i, j: (j, 0), layer),
            pl.BlockSpec((1, d), lambda i, j: (0, 0)),
        ],
        out_specs=pl.BlockSpec((tm, d), lambda i, j: (i, 0)),
        out_shape=jax.ShapeDtypeStruct((m, d), jnp.float32),
        scratch_shapes=[
            pltpu.VMEM((tm + HALO, d), jnp.bfloat16),
            pltpu.VMEM((tm + 2 * HALO, tf), jnp.float32),
            pltpu.VMEM((tm + 2 * HALO, tf), jnp.float32),
        ],
        compiler_params=_params(2),
        name="conv_ffn",
    )(x, x, x, g_all, w_up_all, w_up_all, cw_all, cb_all, w_down_all, final_g)


def kernel(x, mem, mix_norm_g, ffn_norm_g, mem_norm_g, w_mem_kv, a_w_in, a_w_out, b_w_in,
           b_v_norm_g, b_w_s, b_s_bias, b_w_out, ffn_w_up, ffn_conv_w, ffn_conv_b, ffn_w_down,
           final_norm_g):
    batch, seq, d = x.shape
    assert (seq, d) == (SEQ, D_MODEL)
    bf = jnp.bfloat16
    xs = x.reshape(batch * seq, d)
    mems = mem.reshape(batch * MEM_LEN, d)

    mix_g = mix_norm_g.reshape(DEPTH, 1, d)
    ffn_g = ffn_norm_g.reshape(DEPTH, 1, d)
    mem_g = mem_norm_g.reshape(DEPTH, 1, d)
    final_g = final_norm_g.reshape(1, d)
    v_norm_g = b_v_norm_g.reshape(-1, 1, B_W)
    s_bias_t = jnp.swapaxes(b_s_bias, 1, 2)
    conv_w = ffn_conv_w.reshape(DEPTH, 3, 2, FF).transpose(0, 2, 1, 3)
    conv_b = ffn_conv_b.reshape(DEPTH, 2, 1, FF)
    w_mem_kv, a_w_in, a_w_out, b_w_in, b_w_s, b_w_out, ffn_w_down = (
        w.astype(bf) for w in (w_mem_kv, a_w_in, a_w_out, b_w_in, b_w_s, b_w_out, ffn_w_down))
    ffn_w_up = ffn_w_up.reshape(DEPTH, d, 2 * FF // FFN_TF, FFN_TF).transpose(0, 2, 1, 3).astype(bf)

    for i in range(DEPTH):
        j = i // 2
        kv = _rms_matmul(mems, mem_g, w_mem_kv, i, i, tm=512, tn=1024, out_dtype=bf,
                         name="mem_kv")
        if i % 2 == 0:
            proj = _rms_matmul(xs, mix_g, a_w_in, i, j, tm=1024, tn=1280,
                               out_dtype=jnp.float32, name="in_proj_a")
            tok = _dilated_attention(proj, batch=batch)
            xs = _mix_out_a(xs, tok, proj, kv, a_w_out, j, tm=512)
        else:
            uvq = _rms_matmul(xs, mix_g, b_w_in, i, j, tm=1024, tn=1792,
                              out_dtype=jnp.float32, gelu_cols=2 * B_W, name="in_proj_b")
            xs = _mix_out_b(xs, uvq, kv, v_norm_g, b_w_s, s_bias_t, b_w_out, j, tm=512)
        xs = _conv_ffn(xs, ffn_g, ffn_w_up, conv_w, conv_b, ffn_w_down, final_g, i,
                       tm=FFN_TM, tf=FFN_TF, final_norm=(i == DEPTH - 1))
    return xs.reshape(batch, seq, d)
```

```python
import functools

import numpy as np
import jax
import jax.numpy as jnp
from jax import lax
from jax.experimental import pallas as pl
from jax.experimental.pallas import tpu as pltpu

D_MODEL = 2048
SEQ = 2048
DEPTH = 4
EPS = 1e-6
NEG = -1e30

HEAD_DIM = 128
HEADS_PER_GROUP = 4
A_PATTERNS = ((128, 1), (512, 4), (2048, 16))
A_GROUPS = len(A_PATTERNS)
A_HEADS = HEADS_PER_GROUP * A_GROUPS
A_QKV_W = A_HEADS * HEAD_DIM
GROUP_W = HEADS_PER_GROUP * HEAD_DIM
QBLK = 128

CHUNK = 128
B_GROUPS = 12
B_W = B_GROUPS * 128

MEM_LEN = 256
MEM_HEADS = 4
MEM_W = MEM_HEADS * HEAD_DIM

A_IN = 3 * A_QKV_W + MEM_W
B_IN = 2 * B_W + MEM_W
FF = 5632

VMEM_LIMIT_BYTES = 58 * 1024 * 1024

_SQRT_HALF = 0.7071067811865476


def _params(n_axes):
    return pltpu.CompilerParams(
        dimension_semantics=("arbitrary",) * n_axes,
        vmem_limit_bytes=VMEM_LIMIT_BYTES,
    )


def _rms(x, g):
    y = x * lax.rsqrt(jnp.mean(x * x, axis=-1, keepdims=True) + EPS)
    return y * g


def _gelu(x):
    return 0.5 * x * (1.0 + lax.erf(x * _SQRT_HALF))


def _alibi_slopes():
    return (2.0 ** (-8.0 * (np.arange(A_HEADS) + 1) / A_HEADS)).astype(np.float32)


def _layer_spec(block, index_map, layer):
    return pl.BlockSpec((None,) + block, lambda *g: (layer,) + index_map(*g))


def _rms_matmul_kernel(x_ref, g_ref, w_ref, o_ref, h_ref, *, gelu_cols):
    j = pl.program_id(1)
    tn = o_ref.shape[1]

    @pl.when(j == 0)
    def _():
        h_ref[...] = _rms(x_ref[...], g_ref[...]).astype(jnp.bfloat16)

    acc = jnp.dot(h_ref[...], w_ref[...], preferred_element_type=jnp.float32)
    if gelu_cols:
        col = j * tn + lax.broadcasted_iota(jnp.int32, acc.shape, 1)
        acc = jnp.where(col < gelu_cols, _gelu(acc), acc)
    o_ref[...] = acc.astype(o_ref.dtype)


def _rms_matmul(x, g_all, layer_g, w, *, tm, tn, out_dtype, gelu_cols=0, name):
    m, d = x.shape
    n = w.shape[1]
    assert m % tm == 0 and n % tn == 0
    return pl.pallas_call(
        functools.partial(_rms_matmul_kernel, gelu_cols=gelu_cols),
        grid=(m // tm, n // tn),
        in_specs=[
            pl.BlockSpec((tm, d), lambda i, j: (i, 0)),
            _layer_spec((1, d), lambda i, j: (0, 0), layer_g),
            pl.BlockSpec((d, tn), lambda i, j: (0, j)),
        ],
        out_specs=pl.BlockSpec((tm, tn), lambda i, j: (i, j)),
        out_shape=jax.ShapeDtypeStruct((m, n), out_dtype),
        scratch_shapes=[pltpu.VMEM((tm, d), jnp.bfloat16)],
        compiler_params=_params(2),
        name=name,
    )(x, g_all, w)


def _mem_kv_kernel(x_ref, g_ref, w_ref, o_ref):
    h = _rms(x_ref[...], g_ref[...]).astype(jnp.bfloat16)
    w = w_ref[...].astype(jnp.bfloat16)
    o_ref[...] = jnp.dot(h, w, preferred_element_type=jnp.float32).astype(o_ref.dtype)


def _mem_kv(mems, g_all, w_all):
    m, d = mems.shape
    layers, _, n = w_all.shape
    return pl.pallas_call(
        _mem_kv_kernel,
        grid=(layers,),
        in_specs=[
            pl.BlockSpec((m, d), lambda l: (0, 0)),
            pl.BlockSpec((None, 1, d), lambda l: (l, 0, 0)),
            pl.BlockSpec((None, d, n), lambda l: (l, 0, 0)),
        ],
        out_specs=pl.BlockSpec((None, m, n), lambda l: (l, 0, 0)),
        out_shape=jax.ShapeDtypeStruct((layers, m, n), jnp.bfloat16),
        compiler_params=_params(1),
        name="mem_kv",
    )(mems, g_all, w_all)


def _group_geometry(grp):
    window, dilation = A_PATTERNS[grp]
    n_side = (window // 2) // dilation
    length = SEQ // dilation
    kw = min(length, QBLK + 2 * n_side)
    return dilation, n_side, length, kw


def _key_start(qi, n_side, length, kw):
    return min(max(qi * QBLK - n_side, 0), length - kw)


def _dilated_attn_kernel(slopes_ref, *refs):
    qkv_refs = refs[:3 * A_GROUPS]
    tok_ref = refs[3 * A_GROUPS]
    o_scr, lse_scr, bias_scr = refs[3 * A_GROUPS + 1:3 * A_GROUPS + 4]
    stage = refs[3 * A_GROUPS + 4:]
    head = pl.program_id(1)
    scale = HEAD_DIM ** -0.5

    for grp in range(A_GROUPS):
        dilation, n_side, length, kw = _group_geometry(grp)
        nblk = length // QBLK
        q_ref, k_ref, v_ref = qkv_refs[3 * grp:3 * grp + 3]
        qs_ref, ks_ref, vs_ref = stage[3 * grp:3 * grp + 3]
        slope = slopes_ref[grp, head]

        for r in range(dilation):
            rows = pl.ds(r, length, stride=dilation) if dilation > 1 else slice(None)
            dst = slice(r * length, (r + 1) * length)
            qs_ref[dst, :] = q_ref[rows, :].astype(jnp.bfloat16)
            ks_ref[dst, :] = k_ref[rows, :].astype(jnp.bfloat16)
            vs_ref[dst, 0:HEAD_DIM] = v_ref[rows, :].astype(jnp.bfloat16)
        vs_ref[:, HEAD_DIM:2 * HEAD_DIM] = jnp.ones((SEQ, HEAD_DIM), jnp.bfloat16)

        offsets = sorted({qi * QBLK - _key_start(qi, n_side, length, kw) for qi in range(nblk)})
        for t, off in enumerate(offsets):
            rel = (lax.broadcasted_iota(jnp.int32, (QBLK, kw), 1)
                   - lax.broadcasted_iota(jnp.int32, (QBLK, kw), 0) - off)
            dist = jnp.abs(rel)
            alibi = (-slope) * (dist * dilation).astype(jnp.float32)
            bias_scr[grp, t, :, 0:kw] = jnp.where(dist <= n_side, alibi, NEG)

        for r in range(dilation):
            for qi in range(nblk):
                k0 = _key_start(qi, n_side, length, kw)
                t = offsets.index(qi * QBLK - k0)
                q = qs_ref[r * length + qi * QBLK:r * length + (qi + 1) * QBLK, :]
                k = ks_ref[r * length + k0:r * length + k0 + kw, :]
                v1 = vs_ref[r * length + k0:r * length + k0 + kw, :]
                bias = bias_scr[grp, t, :, 0:kw]
                s = lax.dot_general(q, k, (((1,), (1,)), ((), ())),
                                    preferred_element_type=jnp.float32) * scale
                s = jnp.where(bias > 0.5 * NEG, s + bias, NEG)
                m = jnp.max(s, axis=-1, keepdims=True)
                p = jnp.exp(s - m).astype(jnp.bfloat16)
                ol = jnp.dot(p, v1, preferred_element_type=jnp.float32)
                l = ol[:, HEAD_DIM:]
                start = qi * QBLK * dilation + r
                dst = pl.ds(start, QBLK, stride=dilation) if dilation > 1 else pl.ds(start, QBLK)
                o_scr[grp, dst, :] = ol[:, :HEAD_DIM] / l
                lse_scr[grp, dst, :] = m + jnp.log(l)

    rows_per_step = 256
    for c in range(SEQ // rows_per_step):
        rows = slice(c * rows_per_step, (c + 1) * rows_per_step)
        lses = [lse_scr[grp, rows, :] for grp in range(A_GROUPS)]
        mx = functools.reduce(jnp.maximum, lses)
        es = [jnp.exp(l - mx) for l in lses]
        den = functools.reduce(lambda a, b: a + b, es)
        tok = sum((e / den) * o_scr[grp, rows, :] for grp, e in enumerate(es))
        tok_ref[rows, :] = tok.astype(tok_ref.dtype)


def _dilated_attention(proj, *, batch):
    slopes = jnp.asarray(_alibi_slopes().reshape(A_GROUPS, HEADS_PER_GROUP))

    def col(which, grp):
        base = which * A_HEADS + grp * HEADS_PER_GROUP
        return lambda b, h, sl: (b, base + h)

    in_specs = [pl.BlockSpec((SEQ, HEAD_DIM), col(which, grp))
                for grp in range(A_GROUPS) for which in range(3)]
    max_kw = max(_group_geometry(grp)[3] for grp in range(A_GROUPS))
    stage = []
    for _ in range(A_GROUPS):
        stage += [pltpu.VMEM((SEQ, HEAD_DIM), jnp.bfloat16),
                  pltpu.VMEM((SEQ, HEAD_DIM), jnp.bfloat16),
                  pltpu.VMEM((SEQ, 2 * HEAD_DIM), jnp.bfloat16)]
    return pl.pallas_call(
        _dilated_attn_kernel,
        grid_spec=pltpu.PrefetchScalarGridSpec(
            num_scalar_prefetch=1,
            grid=(batch, HEADS_PER_GROUP),
            in_specs=in_specs,
            out_specs=pl.BlockSpec((SEQ, HEAD_DIM), lambda b, h, sl: (b, h)),
            scratch_shapes=[
                pltpu.VMEM((A_GROUPS, SEQ, HEAD_DIM), jnp.float32),
                pltpu.VMEM((A_GROUPS, SEQ, HEAD_DIM), jnp.float32),
                pltpu.VMEM((A_GROUPS, 3, QBLK, max_kw), jnp.float32),
            ] + stage,
        ),
        out_shape=jax.ShapeDtypeStruct((batch * SEQ, GROUP_W), jnp.bfloat16),
        compiler_params=_params(2),
        name="dilated_attn",
    )(slopes, *([proj] * (3 * A_GROUPS)))


def _mem_attention(qm, kv_ref):
    scale = HEAD_DIM ** -0.5
    outs = []
    for h in range(MEM_HEADS):
        q = qm[:, h * HEAD_DIM:(h + 1) * HEAD_DIM]
        k = kv_ref[:, h * HEAD_DIM:(h + 1) * HEAD_DIM]
        v = kv_ref[:, MEM_W + h * HEAD_DIM:MEM_W + (h + 1) * HEAD_DIM]
        s = lax.dot_general(q, k, (((1,), (1,)), ((), ())),
                            preferred_element_type=jnp.float32) * scale
        m = jnp.max(s, axis=-1, keepdims=True)
        p = jnp.exp(s - m)
        l = jnp.sum(p, axis=-1, keepdims=True)
        o = jnp.dot(p.astype(jnp.bfloat16), v, preferred_element_type=jnp.float32)
        outs.append(o / l)
    return jnp.concatenate(outs, axis=-1)


def _mix_out_a_kernel(x_ref, tok_ref, qm_ref, kv_ref, w_ref, out_ref):
    mem_out = _mem_attention(qm_ref[...].astype(jnp.bfloat16), kv_ref).astype(jnp.bfloat16)
    cat = jnp.concatenate([tok_ref[...], mem_out], axis=-1)
    out_ref[...] = x_ref[...] + jnp.dot(cat, w_ref[...], preferred_element_type=jnp.float32)


def _kv_spec(layer, batch, tiles_per_seq):
    return pl.BlockSpec((None, MEM_LEN, 2 * MEM_W),
                        lambda i: (layer * batch + i // tiles_per_seq, 0, 0))


def _mix_out_a(x, tok, proj, kv, w_out, layer, *, tm):
    m, d = x.shape
    tiles_per_seq = SEQ // tm
    batch = m // SEQ
    row = lambda i: (i, 0)
    return pl.pallas_call(
        _mix_out_a_kernel,
        grid=(m // tm,),
        in_specs=[
            pl.BlockSpec((tm, d), row),
            pl.BlockSpec((tm, GROUP_W), row),
            pl.BlockSpec((tm, MEM_W), lambda i: (i, 3 * A_QKV_W // MEM_W)),
            _kv_spec(layer, batch, tiles_per_seq),
            pl.BlockSpec(w_out.shape, lambda i: (0, 0)),
        ],
        out_specs=pl.BlockSpec((tm, d), row),
        out_shape=jax.ShapeDtypeStruct((m, d), jnp.float32),
        compiler_params=_params(1),
        name="mix_out_a",
    )(x, tok, proj, kv, w_out)


def _mix_out_b_kernel(x_ref, u_ref, v_ref, qm_ref, vg_ref, ws_ref, sb_ref, kv_ref, w_ref,
                      out_ref, tok_ref):
    tm = x_ref.shape[0]
    vn = _rms(v_ref[...], vg_ref[...]).astype(jnp.bfloat16)
    for c in range(tm // CHUNK):
        rows = slice(c * CHUNK, (c + 1) * CHUNK)
        for g in range(B_GROUPS):
            cols = slice(g * 128, (g + 1) * 128)
            mixed = jnp.dot(ws_ref[g], vn[rows, cols], preferred_element_type=jnp.float32)
            mixed = mixed + sb_ref[:, g:g + 1]
            tok_ref[rows, cols] = (u_ref[rows, cols] * mixed).astype(jnp.bfloat16)
    mem_out = _mem_attention(qm_ref[...].astype(jnp.bfloat16), kv_ref).astype(jnp.bfloat16)
    cat = jnp.concatenate([tok_ref[...], mem_out], axis=-1)
    out_ref[...] = x_ref[...] + jnp.dot(cat, w_ref[...], preferred_element_type=jnp.float32)


def _mix_out_b(x, uvq, kv, v_norm_g_all, w_s_all, s_bias_t_all, w_out, layer, layer_b, *, tm):
    m, d = x.shape
    tiles_per_seq = SEQ // tm
    batch = m // SEQ
    row = lambda i: (i, 0)
    const2 = lambda i: (0, 0)
    return pl.pallas_call(
        _mix_out_b_kernel,
        grid=(m // tm,),
        in_specs=[
            pl.BlockSpec((tm, d), row),
            pl.BlockSpec((tm, B_W), row),
            pl.BlockSpec((tm, B_W), lambda i: (i, 1)),
            pl.BlockSpec((tm, MEM_W), lambda i: (i, 2 * B_W // MEM_W)),
            _layer_spec((1, B_W), const2, layer_b),
            _layer_spec(w_s_all.shape[1:], lambda i: (0, 0, 0), layer_b),
            _layer_spec(s_bias_t_all.shape[1:], const2, layer_b),
            _kv_spec(layer, batch, tiles_per_seq),
            pl.BlockSpec(w_out.shape, const2),
        ],
        out_specs=pl.BlockSpec((tm, d), row),
        out_shape=jax.ShapeDtypeStruct((m, d), jnp.float32),
        scratch_shapes=[pltpu.VMEM((tm, B_W), jnp.bfloat16)],
        compiler_params=_params(1),
        name="mix_out_b",
    )(x, uvq, uvq, uvq, v_norm_g_all, w_s_all, s_bias_t_all, kv, w_out)


HALO = 16
FFN_TM = 512
FFN_TF = 512


N_FFN_IN = 10


def _conv_ffn_kernel(*refs, cast_modes, tiles_per_seq, final_norm):
    n_casts = len(cast_modes)
    (x_ref, xp_ref, xn_ref, g_ref, wg_ref, wv_ref, cw_ref, cb_ref, wd_ref,
     fg_ref) = refs[:N_FFN_IN]
    cast_in = refs[N_FFN_IN:N_FFN_IN + n_casts]
    out_ref = refs[N_FFN_IN + n_casts]
    cast_out = refs[N_FFN_IN + n_casts + 1:N_FFN_IN + 2 * n_casts + 1]
    h_ref, ag_ref, av_ref = refs[N_FFN_IN + 2 * n_casts + 1:]
    i = pl.program_id(0)
    j = pl.program_id(1)
    nj = pl.num_programs(1)
    tm = x_ref.shape[0]

    def cast(mode):
        for src, dst, m in zip(cast_in, cast_out, cast_modes):
            if m == mode:
                dst[...] = src[...].astype(jnp.bfloat16)

    @pl.when(j == 0)
    def _():
        g = g_ref[...]
        h_ref[0:tm, :] = _rms(x_ref[...], g).astype(jnp.bfloat16)
        first = (i % tiles_per_seq) == 0
        last = (i % tiles_per_seq) == tiles_per_seq - 1
        r = lax.broadcasted_iota(jnp.int32, (HALO, 1), 0)
        take_next = jnp.logical_and(r == 0, jnp.logical_not(last))
        take_prev = jnp.logical_and(r == HALO - 1, jnp.logical_not(first))
        halo = jnp.where(take_next, _rms(xn_ref[...], g),
                         jnp.where(take_prev, _rms(xp_ref[...], g), 0.0))
        h_ref[tm:tm + HALO, :] = halo.astype(jnp.bfloat16)
        out_ref[...] = x_ref[...]
        cast("tile")

    cast("step")
    h = h_ref[...]

    def conv(w_ref, a_ref, half):
        a = jnp.dot(h, w_ref[...], preferred_element_type=jnp.float32)
        a_ref[HALO:HALO + tm, :] = a[0:tm]
        a_ref[0:HALO, :] = a[tm:tm + HALO]
        a_ref[HALO + tm:2 * HALO + tm, :] = a[tm:tm + HALO]
        cw = cw_ref[half]
        return (a_ref[HALO - 1:HALO - 1 + tm, :] * cw[0:1]
                + a_ref[HALO:HALO + tm, :] * cw[1:2]
                + a_ref[HALO + 1:HALO + 1 + tm, :] * cw[2:3]
                + cb_ref[half])

    gate = conv(wg_ref, ag_ref, 0)
    val = conv(wv_ref, av_ref, 1)
    act = (_gelu(gate) * val).astype(jnp.bfloat16)
    out_ref[...] += jnp.dot(act, wd_ref[...], preferred_element_type=jnp.float32)

    if final_norm:
        @pl.when(j == nj - 1)
        def _():
            out_ref[...] = _rms(out_ref[...], fg_ref[...])


def _conv_ffn(x, g_all, w_up, w_down, cw_all, cb_all, final_g, layer, casts, *, tm, tf,
              final_norm):
    m, d = x.shape
    nf = FF // tf
    assert m % tm == 0 and FF % tf == 0 and SEQ % tm == 0 and tm % HALO == 0
    n_tiles = m // tm
    tiles_per_seq = SEQ // tm
    hb = tm // HALO
    n_hblocks = m // HALO

    cast_in_specs, cast_out_specs, cast_out_shapes, cast_modes = [], [], [], []
    for src, src_layer, split in casts:
        _, rows, width = src.shape
        if split == "tile":
            block, imap = (rows // n_tiles, width), (lambda i, j: (i, 0))
        elif split == "tile_chunk":
            block, imap = (rows // n_tiles, width // nf), (lambda i, j: (i, j))
        else:
            block, imap = (rows // (n_tiles * nf), width), (lambda i, j: (i * nf + j, 0))
        assert block[0] % 16 == 0 and block[1] % 128 == 0
        assert rows % block[0] == 0 and width % block[1] == 0
        cast_in_specs.append(_layer_spec(block, imap, src_layer))
        cast_out_specs.append(pl.BlockSpec(block, imap))
        cast_out_shapes.append(jax.ShapeDtypeStruct((rows, width), jnp.bfloat16))
        cast_modes.append("tile" if split == "tile" else "step")

    outs = pl.pallas_call(
        functools.partial(_conv_ffn_kernel, cast_modes=tuple(cast_modes),
                          tiles_per_seq=tiles_per_seq, final_norm=final_norm),
        grid=(n_tiles, nf),
        in_specs=[
            pl.BlockSpec((tm, d), lambda i, j: (i, 0)),
            pl.BlockSpec((HALO, d), lambda i, j: (jnp.maximum(i * hb - 1, 0), 0)),
            pl.BlockSpec((HALO, d), lambda i, j: (jnp.minimum((i + 1) * hb, n_hblocks - 1), 0)),
            _layer_spec((1, d), lambda i, j: (0, 0), layer),
            pl.BlockSpec((d, tf), lambda i, j: (0, j)),
            pl.BlockSpec((d, tf), lambda i, j: (0, nf + j)),
            _layer_spec((2, 3, tf), lambda i, j: (0, 0, j), layer),
            _layer_spec((2, 1, tf), lambda i, j: (0, 0, j), layer),
            pl.BlockSpec((tf, d), lambda i, j: (j, 0)),
            pl.BlockSpec((1, d), lambda i, j: (0, 0)),
        ] + cast_in_specs,
        out_specs=[pl.BlockSpec((tm, d), lambda i, j: (i, 0))] + cast_out_specs,
        out_shape=[jax.ShapeDtypeStruct((m, d), jnp.float32)] + cast_out_shapes,
        scratch_shapes=[
            pltpu.VMEM((tm + HALO, d), jnp.bfloat16),
            pltpu.VMEM((tm + 2 * HALO, tf), jnp.float32),
            pltpu.VMEM((tm + 2 * HALO, tf), jnp.float32),
        ],
        compiler_params=_params(2),
        name="conv_ffn",
    )(x, x, x, g_all, w_up, w_up, cw_all, cb_all, w_down, final_g, *[c[0] for c in casts])
    return outs[0], outs[1:]


def kernel(x, mem, mix_norm_g, ffn_norm_g, mem_norm_g, w_mem_kv, a_w_in, a_w_out, b_w_in,
           b_v_norm_g, b_w_s, b_s_bias, b_w_out, ffn_w_up, ffn_conv_w, ffn_conv_b, ffn_w_down,
           final_norm_g):
    batch, seq, d = x.shape
    assert (seq, d) == (SEQ, D_MODEL)
    bf = jnp.bfloat16
    xs = x.reshape(batch * seq, d)
    mems = mem.reshape(batch * MEM_LEN, d)

    mix_g = mix_norm_g.reshape(DEPTH, 1, d)
    ffn_g = ffn_norm_g.reshape(DEPTH, 1, d)
    mem_g = mem_norm_g.reshape(DEPTH, 1, d)
    final_g = final_norm_g.reshape(1, d)
    v_norm_g = b_v_norm_g.reshape(-1, 1, B_W)
    s_bias_t = jnp.swapaxes(b_s_bias, 1, 2)
    conv_w = ffn_conv_w.reshape(DEPTH, 3, 2, FF).transpose(0, 2, 1, 3)
    conv_b = ffn_conv_b.reshape(DEPTH, 2, 1, FF)
    b_w_s = b_w_s.astype(bf)
    kv = _mem_kv(mems, mem_g, w_mem_kv).reshape(DEPTH * batch, MEM_LEN, 2 * MEM_W)

    def mixer_weights(i):
        return (a_w_in, a_w_out) if i % 2 == 0 else (b_w_in, b_w_out)

    w_in, w_out = (w[0].astype(bf) for w in mixer_weights(0))
    w_up, w_down = ffn_w_up[0].astype(bf), ffn_w_down[0].astype(bf)

    for i in range(DEPTH):
        j = i // 2
        if i % 2 == 0:
            proj = _rms_matmul(xs, mix_g, i, w_in, tm=1024, tn=1280, out_dtype=jnp.float32,
                               name="in_proj_a")
            tok = _dilated_attention(proj, batch=batch)
            xs = _mix_out_a(xs, tok, proj, kv, w_out, i, tm=512)
        else:
            uvq = _rms_matmul(xs, mix_g, i, w_in, tm=1024, tn=1792, out_dtype=jnp.float32,
                              gelu_cols=2 * B_W, name="in_proj_b")
            xs = _mix_out_b(xs, uvq, kv, v_norm_g, b_w_s, s_bias_t, w_out, i, j, tm=512)
        casts = []
        if i + 1 < DEPTH:
            nxt_in, nxt_out = mixer_weights(i + 1)
            casts = [(nxt_in, (i + 1) // 2, "tile"), (nxt_out, (i + 1) // 2, "tile"),
                     (ffn_w_up, i + 1, "tile_chunk"), (ffn_w_down, i + 1, "step")]
        xs, nxt = _conv_ffn(xs, ffn_g, w_up, w_down, conv_w, conv_b, final_g, i, casts,
                            tm=FFN_TM, tf=FFN_TF, final_norm=(i == DEPTH - 1))
        if nxt:
            w_in, w_out, w_up, w_down = nxt
    return xs.reshape(batch, seq, d)
```

```python
import functools

import numpy as np
import jax
import jax.numpy as jnp
from jax import lax
from jax.experimental import pallas as pl
from jax.experimental.pallas import tpu as pltpu

D_MODEL = 2048
SEQ = 2048
DEPTH = 4
EPS = 1e-6
NEG = -1e30

HEAD_DIM = 128
HEADS_PER_GROUP = 4
A_PATTERNS = ((128, 1), (512, 4), (2048, 16))
A_GROUPS = len(A_PATTERNS)
A_HEADS = HEADS_PER_GROUP * A_GROUPS
A_QKV_W = A_HEADS * HEAD_DIM
GROUP_W = HEADS_PER_GROUP * HEAD_DIM
QBLK = 128

CHUNK = 128
B_GROUPS = 12
B_W = B_GROUPS * 128

MEM_LEN = 256
MEM_HEADS = 4
MEM_W = MEM_HEADS * HEAD_DIM

A_IN = 3 * A_QKV_W + MEM_W
B_IN = 2 * B_W + MEM_W
FF = 5632

VMEM_LIMIT_BYTES = 58 * 1024 * 1024

_SQRT_HALF = 0.7071067811865476


def _params(n_axes):
    return pltpu.CompilerParams(
        dimension_semantics=("arbitrary",) * n_axes,
        vmem_limit_bytes=VMEM_LIMIT_BYTES,
    )


def _rms(x, g):
    y = x * lax.rsqrt(jnp.mean(x * x, axis=-1, keepdims=True) + EPS)
    return y * g


NORM_ROWS = 16


def _rms_to(dst_ref, x_ref, g, copy_ref=None):
    n = x_ref.shape[0]
    for r0 in range(0, n, NORM_ROWS):
        rows = slice(r0, r0 + NORM_ROWS)
        x = x_ref[rows, :]
        dst_ref[rows, :] = _rms(x, g).astype(dst_ref.dtype)
        if copy_ref is not None:
            copy_ref[rows, :] = x


def _gelu(x):
    return 0.5 * x * (1.0 + lax.erf(x * _SQRT_HALF))


def _alibi_slopes():
    return (2.0 ** (-8.0 * (np.arange(A_HEADS) + 1) / A_HEADS)).astype(np.float32)


def _layer_spec(block, index_map, layer):
    return pl.BlockSpec((None,) + block, lambda *g: (layer,) + index_map(*g))


def _rms_matmul_kernel(x_ref, g_ref, w_ref, o_ref, h_ref, *, gelu_cols, head_major):
    j = pl.program_id(1)
    tn = w_ref.shape[1]

    @pl.when(j == 0)
    def _():
        _rms_to(h_ref, x_ref, g_ref[...])

    acc = jnp.dot(h_ref[...], w_ref[...], preferred_element_type=jnp.float32)
    if gelu_cols:
        col = j * tn + lax.broadcasted_iota(jnp.int32, acc.shape, 1)
        acc = jnp.where(col < gelu_cols, _gelu(acc), acc)
    if head_major:
        for c in range(tn // HEAD_DIM):
            o_ref[c] = acc[:, c * HEAD_DIM:(c + 1) * HEAD_DIM].astype(o_ref.dtype)
    else:
        o_ref[...] = acc.astype(o_ref.dtype)


def _rms_matmul(x, g_all, layer_g, w, *, tm, tn, out_dtype, gelu_cols=0, head_major=False,
                name):
    m, d = x.shape
    n = w.shape[1]
    assert m % tm == 0 and n % tn == 0
    if head_major:
        out_spec = pl.BlockSpec((tn // HEAD_DIM, tm, HEAD_DIM), lambda i, j: (j, i, 0))
        out_shape = jax.ShapeDtypeStruct((n // HEAD_DIM, m, HEAD_DIM), out_dtype)
    else:
        out_spec = pl.BlockSpec((tm, tn), lambda i, j: (i, j))
        out_shape = jax.ShapeDtypeStruct((m, n), out_dtype)
    return pl.pallas_call(
        functools.partial(_rms_matmul_kernel, gelu_cols=gelu_cols, head_major=head_major),
        grid=(m // tm, n // tn),
        in_specs=[
            pl.BlockSpec((tm, d), lambda i, j: (i, 0)),
            _layer_spec((1, d), lambda i, j: (0, 0), layer_g),
            pl.BlockSpec((d, tn), lambda i, j: (0, j)),
        ],
        out_specs=out_spec,
        out_shape=out_shape,
        scratch_shapes=[pltpu.VMEM((tm, d), jnp.bfloat16)],
        compiler_params=_params(2),
        name=name,
    )(x, g_all, w)


def _mem_kv_kernel(x_ref, g_ref, w_ref, o_ref):
    h = _rms(x_ref[...], g_ref[...]).astype(jnp.bfloat16)
    w = w_ref[...].astype(jnp.bfloat16)
    o_ref[...] = jnp.dot(h, w, preferred_element_type=jnp.float32).astype(o_ref.dtype)


def _mem_kv(mems, g_all, w_all):
    m, d = mems.shape
    layers, _, n = w_all.shape
    return pl.pallas_call(
        _mem_kv_kernel,
        grid=(layers,),
        in_specs=[
            pl.BlockSpec((m, d), lambda l: (0, 0)),
            pl.BlockSpec((None, 1, d), lambda l: (l, 0, 0)),
            pl.BlockSpec((None, d, n), lambda l: (l, 0, 0)),
        ],
        out_specs=pl.BlockSpec((None, m, n), lambda l: (l, 0, 0)),
        out_shape=jax.ShapeDtypeStruct((layers, m, n), jnp.bfloat16),
        compiler_params=_params(1),
        name="mem_kv",
    )(mems, g_all, w_all)


def _group_geometry(grp):
    window, dilation = A_PATTERNS[grp]
    n_side = (window // 2) // dilation
    length = SEQ // dilation
    kw = min(length, QBLK + 2 * n_side)
    return dilation, n_side, length, kw


def _key_start(qi, n_side, length, kw):
    return min(max(qi * QBLK - n_side, 0), length - kw)


def _dilated_attn_kernel(slopes_ref, *refs):
    qkv_refs = refs[:3 * A_GROUPS]
    tok_ref = refs[3 * A_GROUPS]
    o_scr, lse_scr, bias_scr = refs[3 * A_GROUPS + 1:3 * A_GROUPS + 4]
    stage = refs[3 * A_GROUPS + 4:]
    head = pl.program_id(1)
    scale = HEAD_DIM ** -0.5

    for grp in range(A_GROUPS):
        dilation, n_side, length, kw = _group_geometry(grp)
        nblk = length // QBLK
        q_ref, k_ref, v_ref = qkv_refs[3 * grp:3 * grp + 3]
        qs_ref, ks_ref, vs_ref = stage[3 * grp:3 * grp + 3]
        slope = slopes_ref[grp, head]

        for r in range(dilation):
            rows = pl.ds(r, length, stride=dilation) if dilation > 1 else slice(None)
            dst = slice(r * length, (r + 1) * length)
            qs_ref[dst, :] = q_ref[rows, :].astype(jnp.bfloat16)
            ks_ref[dst, :] = k_ref[rows, :].astype(jnp.bfloat16)
            vs_ref[dst, 0:HEAD_DIM] = v_ref[rows, :].astype(jnp.bfloat16)
        vs_ref[:, HEAD_DIM:2 * HEAD_DIM] = jnp.ones((SEQ, HEAD_DIM), jnp.bfloat16)

        offsets = sorted({qi * QBLK - _key_start(qi, n_side, length, kw) for qi in range(nblk)})
        for t, off in enumerate(offsets):
            rel = (lax.broadcasted_iota(jnp.int32, (QBLK, kw), 1)
                   - lax.broadcasted_iota(jnp.int32, (QBLK, kw), 0) - off)
            dist = jnp.abs(rel)
            alibi = (-slope) * (dist * dilation).astype(jnp.float32)
            bias_scr[grp, t, :, 0:kw] = jnp.where(dist <= n_side, alibi, NEG)

        for r in range(dilation):
            for qi in range(nblk):
                k0 = _key_start(qi, n_side, length, kw)
                t = offsets.index(qi * QBLK - k0)
                q = qs_ref[r * length + qi * QBLK:r * length + (qi + 1) * QBLK, :]
                k = ks_ref[r * length + k0:r * length + k0 + kw, :]
                v1 = vs_ref[r * length + k0:r * length + k0 + kw, :]
                bias = bias_scr[grp, t, :, 0:kw]
                s = lax.dot_general(q, k, (((1,), (1,)), ((), ())),
                                    preferred_element_type=jnp.float32) * scale
                s = jnp.where(bias > 0.5 * NEG, s + bias, NEG)
                m = jnp.max(s, axis=-1, keepdims=True)
                p = jnp.exp(s - m).astype(jnp.bfloat16)
                ol = jnp.dot(p, v1, preferred_element_type=jnp.float32)
                l = ol[:, HEAD_DIM:]
                start = qi * QBLK * dilation + r
                dst = pl.ds(start, QBLK, stride=dilation) if dilation > 1 else pl.ds(start, QBLK)
                o_scr[grp, dst, :] = ol[:, :HEAD_DIM] / l
                lse_scr[grp, dst, :] = m + jnp.log(l)

    rows_per_step = 256
    for c in range(SEQ // rows_per_step):
        rows = slice(c * rows_per_step, (c + 1) * rows_per_step)
        lses = [lse_scr[grp, rows, :] for grp in range(A_GROUPS)]
        mx = functools.reduce(jnp.maximum, lses)
        es = [jnp.exp(l - mx) for l in lses]
        den = functools.reduce(lambda a, b: a + b, es)
        tok = sum((e / den) * o_scr[grp, rows, :] for grp, e in enumerate(es))
        tok_ref[rows, :] = tok.astype(tok_ref.dtype)


def _dilated_attention(proj, *, batch):
    slopes = jnp.asarray(_alibi_slopes().reshape(A_GROUPS, HEADS_PER_GROUP))

    def col(which, grp):
        base = which * A_HEADS + grp * HEADS_PER_GROUP
        return lambda b, h, sl: (base + h, b, 0)

    in_specs = [pl.BlockSpec((None, SEQ, HEAD_DIM), col(which, grp))
                for grp in range(A_GROUPS) for which in range(3)]
    max_kw = max(_group_geometry(grp)[3] for grp in range(A_GROUPS))
    stage = []
    for _ in range(A_GROUPS):
        stage += [pltpu.VMEM((SEQ, HEAD_DIM), jnp.bfloat16),
                  pltpu.VMEM((SEQ, HEAD_DIM), jnp.bfloat16),
                  pltpu.VMEM((SEQ, 2 * HEAD_DIM), jnp.bfloat16)]
    return pl.pallas_call(
        _dilated_attn_kernel,
        grid_spec=pltpu.PrefetchScalarGridSpec(
            num_scalar_prefetch=1,
            grid=(batch, HEADS_PER_GROUP),
            in_specs=in_specs,
            out_specs=pl.BlockSpec((SEQ, HEAD_DIM), lambda b, h, sl: (b, h)),
            scratch_shapes=[
                pltpu.VMEM((A_GROUPS, SEQ, HEAD_DIM), jnp.float32),
                pltpu.VMEM((A_GROUPS, SEQ, HEAD_DIM), jnp.float32),
                pltpu.VMEM((A_GROUPS, 3, QBLK, max_kw), jnp.float32),
            ] + stage,
        ),
        out_shape=jax.ShapeDtypeStruct((batch * SEQ, GROUP_W), jnp.bfloat16),
        compiler_params=_params(2),
        name="dilated_attn",
    )(slopes, *([proj] * (3 * A_GROUPS)))


def _mem_attention(q_heads, kv_ref):
    scale = HEAD_DIM ** -0.5
    outs = []
    for h, q in enumerate(q_heads):
        k = kv_ref[:, h * HEAD_DIM:(h + 1) * HEAD_DIM]
        v = kv_ref[:, MEM_W + h * HEAD_DIM:MEM_W + (h + 1) * HEAD_DIM]
        s = lax.dot_general(q, k, (((1,), (1,)), ((), ())),
                            preferred_element_type=jnp.float32) * scale
        m = jnp.max(s, axis=-1, keepdims=True)
        p = jnp.exp(s - m)
        l = jnp.sum(p, axis=-1, keepdims=True)
        o = jnp.dot(p.astype(jnp.bfloat16), v, preferred_element_type=jnp.float32)
        outs.append(o / l)
    return jnp.concatenate(outs, axis=-1)


def _mix_out_a_kernel(x_ref, tok_ref, qm_ref, kv_ref, w_ref, out_ref):
    q_heads = [qm_ref[h].astype(jnp.bfloat16) for h in range(MEM_HEADS)]
    mem_out = _mem_attention(q_heads, kv_ref).astype(jnp.bfloat16)
    cat = jnp.concatenate([tok_ref[...], mem_out], axis=-1)
    out_ref[...] = x_ref[...] + jnp.dot(cat, w_ref[...], preferred_element_type=jnp.float32)


def _kv_spec(layer, batch, tiles_per_seq):
    return pl.BlockSpec((None, MEM_LEN, 2 * MEM_W),
                        lambda i: (layer * batch + i // tiles_per_seq, 0, 0))


def _mix_out_a(x, tok, proj, kv, w_out, layer, *, tm):
    m, d = x.shape
    tiles_per_seq = SEQ // tm
    batch = m // SEQ
    row = lambda i: (i, 0)
    return pl.pallas_call(
        _mix_out_a_kernel,
        grid=(m // tm,),
        in_specs=[
            pl.BlockSpec((tm, d), row),
            pl.BlockSpec((tm, GROUP_W), row),
            pl.BlockSpec((MEM_HEADS, tm, HEAD_DIM), lambda i: (3 * A_HEADS // MEM_HEADS, i, 0)),
            _kv_spec(layer, batch, tiles_per_seq),
            pl.BlockSpec(w_out.shape, lambda i: (0, 0)),
        ],
        out_specs=pl.BlockSpec((tm, d), row),
        out_shape=jax.ShapeDtypeStruct((m, d), jnp.float32),
        compiler_params=_params(1),
        name="mix_out_a",
    )(x, tok, proj, kv, w_out)


def _mix_out_b_kernel(x_ref, u_ref, v_ref, qm_ref, vg_ref, ws_ref, sb_ref, kv_ref, w_ref,
                      out_ref, tok_ref):
    tm = x_ref.shape[0]
    vn = _rms(v_ref[...], vg_ref[...]).astype(jnp.bfloat16)
    for c in range(tm // CHUNK):
        rows = slice(c * CHUNK, (c + 1) * CHUNK)
        for g in range(B_GROUPS):
            cols = slice(g * 128, (g + 1) * 128)
            mixed = jnp.dot(ws_ref[g], vn[rows, cols], preferred_element_type=jnp.float32)
            mixed = mixed + sb_ref[:, g:g + 1]
            tok_ref[rows, cols] = (u_ref[rows, cols] * mixed).astype(jnp.bfloat16)
    qm = qm_ref[...].astype(jnp.bfloat16)
    q_heads = [qm[:, h * HEAD_DIM:(h + 1) * HEAD_DIM] for h in range(MEM_HEADS)]
    mem_out = _mem_attention(q_heads, kv_ref).astype(jnp.bfloat16)
    cat = jnp.concatenate([tok_ref[...], mem_out], axis=-1)
    out_ref[...] = x_ref[...] + jnp.dot(cat, w_ref[...], preferred_element_type=jnp.float32)


def _mix_out_b(x, uvq, kv, v_norm_g_all, w_s_all, s_bias_t_all, w_out, layer, layer_b, *, tm):
    m, d = x.shape
    tiles_per_seq = SEQ // tm
    batch = m // SEQ
    row = lambda i: (i, 0)
    const2 = lambda i: (0, 0)
    return pl.pallas_call(
        _mix_out_b_kernel,
        grid=(m // tm,),
        in_specs=[
            pl.BlockSpec((tm, d), row),
            pl.BlockSpec((tm, B_W), row),
            pl.BlockSpec((tm, B_W), lambda i: (i, 1)),
            pl.BlockSpec((tm, MEM_W), lambda i: (i, 2 * B_W // MEM_W)),
            _layer_spec((1, B_W), const2, layer_b),
            _layer_spec(w_s_all.shape[1:], lambda i: (0, 0, 0), layer_b),
            _layer_spec(s_bias_t_all.shape[1:], const2, layer_b),
            _kv_spec(layer, batch, tiles_per_seq),
            pl.BlockSpec(w_out.shape, const2),
        ],
        out_specs=pl.BlockSpec((tm, d), row),
        out_shape=jax.ShapeDtypeStruct((m, d), jnp.float32),
        scratch_shapes=[pltpu.VMEM((tm, B_W), jnp.bfloat16)],
        compiler_params=_params(1),
        name="mix_out_b",
    )(x, uvq, uvq, uvq, v_norm_g_all, w_s_all, s_bias_t_all, kv, w_out)


BF16_ROWS = 16
HALO = BF16_ROWS
FFN_TM = 512
FFN_TF = 512
MXU_N = 256


N_FFN_IN = 9
CONV_TAPS = 3


def _conv_ffn_kernel(*refs, n_casts, tiles_per_seq, final_norm):
    (x_ref, xp_ref, xn_ref, g_ref, wg_ref, wv_ref, cp_ref, wd_ref,
     fg_ref) = refs[:N_FFN_IN]
    cast_in = refs[N_FFN_IN:N_FFN_IN + n_casts]
    out_ref = refs[N_FFN_IN + n_casts]
    cast_out = refs[N_FFN_IN + n_casts + 1:N_FFN_IN + 2 * n_casts + 1]
    h_ref, ag_ref, av_ref = refs[N_FFN_IN + 2 * n_casts + 1:]
    i = pl.program_id(0)
    j = pl.program_id(1)
    nj = pl.num_programs(1)
    tm = x_ref.shape[0]

    @pl.when(j == 0)
    def _():
        g = g_ref[...]
        _rms_to(h_ref, x_ref, g, copy_ref=out_ref)
        first = (i % tiles_per_seq) == 0
        last = (i % tiles_per_seq) == tiles_per_seq - 1
        r = lax.broadcasted_iota(jnp.int32, (HALO, 1), 0)
        take_next = jnp.logical_and(r == 0, jnp.logical_not(last))
        take_prev = jnp.logical_and(r == HALO - 1, jnp.logical_not(first))
        halo = jnp.where(take_next, _rms(xn_ref[...], g),
                         jnp.where(take_prev, _rms(xp_ref[...], g), 0.0))
        h_ref[tm:tm + HALO, :] = halo.astype(jnp.bfloat16)

    for src, dst in zip(cast_in, cast_out):
        dst[...] = src[...].astype(jnp.bfloat16)
    h = h_ref[...]

    def up(w_ref, a_ref):
        a = jnp.dot(h, w_ref[...], preferred_element_type=jnp.float32)
        a_ref[HALO:HALO + tm, :] = a[0:tm]
        a_ref[0:HALO, :] = a[tm:tm + HALO]
        a_ref[HALO + tm:2 * HALO + tm, :] = a[tm:tm + HALO]

    def conv(a_ref, half, cols):
        cp = cp_ref[half]
        return (a_ref[HALO - 1:HALO - 1 + tm, cols] * cp[0:1, cols]
                + a_ref[HALO:HALO + tm, cols] * cp[1:2, cols]
                + a_ref[HALO + 1:HALO + 1 + tm, cols] * cp[2:3, cols]
                + cp[CONV_TAPS:CONV_TAPS + 1, cols])

    up(wg_ref, ag_ref)
    up(wv_ref, av_ref)
    tf = wg_ref.shape[1]
    upd = None
    for c in range(tf // MXU_N):
        cols = slice(c * MXU_N, (c + 1) * MXU_N)
        act = (_gelu(conv(ag_ref, 0, cols)) * conv(av_ref, 1, cols)).astype(jnp.bfloat16)
        part = jnp.dot(act, wd_ref[cols, :], preferred_element_type=jnp.float32)
        upd = part if upd is None else upd + part
    out_ref[...] += upd

    if final_norm:
        @pl.when(j == nj - 1)
        def _():
            out_ref[...] = _rms(out_ref[...], fg_ref[...])


def _conv_ffn(x, g_all, w_up, w_down, cp_all, final_g, layer, casts, *, tm, tf, final_norm):
    m, d = x.shape
    nf = FF // tf
    assert m % tm == 0 and FF % tf == 0 and SEQ % tm == 0 and tm % HALO == 0
    n_tiles = m // tm
    n_steps = n_tiles * nf
    tiles_per_seq = SEQ // tm
    hb = tm // HALO
    n_hblocks = m // HALO

    cast_in_specs, cast_out_specs, cast_out_shapes = [], [], []
    for src, src_layer in casts:
        _, rows, width = src.shape
        block_rows = BF16_ROWS * pl.cdiv(rows, BF16_ROWS * n_steps)
        n_blocks = rows // block_rows
        assert rows % block_rows == 0 and n_blocks <= n_steps

        def imap(i, j, n_blocks=n_blocks):
            return (jnp.minimum(i * nf + j, n_blocks - 1), 0)

        cast_in_specs.append(_layer_spec((block_rows, width), imap, src_layer))
        cast_out_specs.append(pl.BlockSpec((block_rows, width), imap))
        cast_out_shapes.append(jax.ShapeDtypeStruct((rows, width), jnp.bfloat16))

    outs = pl.pallas_call(
        functools.partial(_conv_ffn_kernel, n_casts=len(casts),
                          tiles_per_seq=tiles_per_seq, final_norm=final_norm),
        grid=(n_tiles, nf),
        in_specs=[
            pl.BlockSpec((tm, d), lambda i, j: (i, 0)),
            pl.BlockSpec((HALO, d), lambda i, j: (jnp.maximum(i * hb - 1, 0), 0)),
            pl.BlockSpec((HALO, d), lambda i, j: (jnp.minimum((i + 1) * hb, n_hblocks - 1), 0)),
            _layer_spec((1, d), lambda i, j: (0, 0), layer),
            pl.BlockSpec((d, tf), lambda i, j: (0, j)),
            pl.BlockSpec((d, tf), lambda i, j: (0, nf + j)),
            _layer_spec((2, CONV_TAPS + 1, tf), lambda i, j: (0, 0, j), layer),
            pl.BlockSpec((tf, d), lambda i, j: (j, 0)),
            pl.BlockSpec((1, d), lambda i, j: (0, 0)),
        ] + cast_in_specs,
        out_specs=[pl.BlockSpec((tm, d), lambda i, j: (i, 0))] + cast_out_specs,
        out_shape=[jax.ShapeDtypeStruct((m, d), jnp.float32)] + cast_out_shapes,
        scratch_shapes=[
            pltpu.VMEM((tm + HALO, d), jnp.bfloat16),
            pltpu.VMEM((tm + 2 * HALO, tf), jnp.float32),
            pltpu.VMEM((tm + 2 * HALO, tf), jnp.float32),
        ],
        compiler_params=_params(2),
        name="conv_ffn",
    )(x, x, x, g_all, w_up, w_up, cp_all, w_down, final_g, *[c[0] for c in casts])
    return outs[0], outs[1:]


def kernel(x, mem, mix_norm_g, ffn_norm_g, mem_norm_g, w_mem_kv, a_w_in, a_w_out, b_w_in,
           b_v_norm_g, b_w_s, b_s_bias, b_w_out, ffn_w_up, ffn_conv_w, ffn_conv_b, ffn_w_down,
           final_norm_g):
    batch, seq, d = x.shape
    assert (seq, d) == (SEQ, D_MODEL)
    bf = jnp.bfloat16
    xs = x.reshape(batch * seq, d)
    mems = mem.reshape(batch * MEM_LEN, d)

    mix_g = mix_norm_g.reshape(DEPTH, 1, d)
    ffn_g = ffn_norm_g.reshape(DEPTH, 1, d)
    mem_g = mem_norm_g.reshape(DEPTH, 1, d)
    final_g = final_norm_g.reshape(1, d)
    v_norm_g = b_v_norm_g.reshape(-1, 1, B_W)
    s_bias_t = jnp.swapaxes(b_s_bias, 1, 2)
    conv_p = jnp.concatenate([ffn_conv_w.reshape(DEPTH, CONV_TAPS, 2, FF).transpose(0, 2, 1, 3),
                              ffn_conv_b.reshape(DEPTH, 2, 1, FF)], axis=2)
    b_w_s = b_w_s.astype(bf)
    kv = _mem_kv(mems, mem_g, w_mem_kv).reshape(DEPTH * batch, MEM_LEN, 2 * MEM_W)

    def mixer_weights(i):
        return (a_w_in, a_w_out) if i % 2 == 0 else (b_w_in, b_w_out)

    w_in, w_out = (w[0].astype(bf) for w in mixer_weights(0))
    w_up, w_down = ffn_w_up[0].astype(bf), ffn_w_down[0].astype(bf)

    for i in range(DEPTH):
        j = i // 2
        if i % 2 == 0:
            proj = _rms_matmul(xs, mix_g, i, w_in, tm=1024, tn=1280, out_dtype=jnp.float32,
                               head_major=True, name="in_proj_a")
            tok = _dilated_attention(proj, batch=batch)
            xs = _mix_out_a(xs, tok, proj, kv, w_out, i, tm=512)
        else:
            uvq = _rms_matmul(xs, mix_g, i, w_in, tm=1024, tn=1792, out_dtype=jnp.float32,
                              gelu_cols=2 * B_W, name="in_proj_b")
            xs = _mix_out_b(xs, uvq, kv, v_norm_g, b_w_s, s_bias_t, w_out, i, j, tm=512)
        casts = []
        if i + 1 < DEPTH:
            nxt_in, nxt_out = mixer_weights(i + 1)
            casts = [(nxt_in, (i + 1) // 2), (nxt_out, (i + 1) // 2), (ffn_w_up, i + 1),
                     (ffn_w_down, i + 1)]
        xs, nxt = _conv_ffn(xs, ffn_g, w_up, w_down, conv_p, final_g, i, casts,
                            tm=FFN_TM, tf=FFN_TF, final_norm=(i == DEPTH - 1))
        if nxt:
            w_in, w_out, w_up, w_down = nxt
    return xs.reshape(batch, seq, d)
```

```python
import functools

import numpy as np
import jax
import jax.numpy as jnp
from jax import lax
from jax.experimental import pallas as pl
from jax.experimental.pallas import tpu as pltpu

D_MODEL = 2048
SEQ = 2048
DEPTH = 4
EPS = 1e-6
NEG = -1e30

HEAD_DIM = 128
HEADS_PER_GROUP = 4
A_PATTERNS = ((128, 1), (512, 4), (2048, 16))
A_GROUPS = len(A_PATTERNS)
A_HEADS = HEADS_PER_GROUP * A_GROUPS
A_QKV_W = A_HEADS * HEAD_DIM
GROUP_W = HEADS_PER_GROUP * HEAD_DIM
QBLK = 128

CHUNK = 128
B_GROUPS = 12
B_W = B_GROUPS * 128

MEM_LEN = 256
MEM_HEADS = 4
MEM_W = MEM_HEADS * HEAD_DIM

A_IN = 3 * A_QKV_W + MEM_W
B_IN = 2 * B_W + MEM_W
FF = 5632

VMEM_LIMIT_BYTES = 62 * 1024 * 1024

_SQRT_HALF = 0.7071067811865476


def _params(n_axes):
    return pltpu.CompilerParams(
        dimension_semantics=("arbitrary",) * n_axes,
        vmem_limit_bytes=VMEM_LIMIT_BYTES,
    )


def _rms(x, g):
    y = x * lax.rsqrt(jnp.mean(x * x, axis=-1, keepdims=True) + EPS)
    return y * g


NORM_ROWS = 16


def _rms_to(dst_ref, x_ref, g, copy_ref=None):
    n = x_ref.shape[0]
    for r0 in range(0, n, NORM_ROWS):
        rows = slice(r0, r0 + NORM_ROWS)
        x = x_ref[rows, :]
        dst_ref[rows, :] = _rms(x, g).astype(dst_ref.dtype)
        if copy_ref is not None:
            copy_ref[rows, :] = x


def _gelu(x):
    return 0.5 * x * (1.0 + lax.erf(x * _SQRT_HALF))


def _alibi_slopes():
    return (2.0 ** (-8.0 * (np.arange(A_HEADS) + 1) / A_HEADS)).astype(np.float32)


def _layer_spec(block, index_map, layer):
    return pl.BlockSpec((None,) + block, lambda *g: (layer,) + index_map(*g))


def _rms_matmul_kernel(x_ref, g_ref, w_ref, o_ref, h_ref, *, gelu_cols, head_major):
    j = pl.program_id(1)
    tn = w_ref.shape[1]

    @pl.when(j == 0)
    def _():
        _rms_to(h_ref, x_ref, g_ref[...])

    acc = jnp.dot(h_ref[...], w_ref[...], preferred_element_type=jnp.float32)
    if gelu_cols:
        col = j * tn + lax.broadcasted_iota(jnp.int32, acc.shape, 1)
        acc = jnp.where(col < gelu_cols, _gelu(acc), acc)
    if head_major:
        for c in range(tn // HEAD_DIM):
            o_ref[c] = acc[:, c * HEAD_DIM:(c + 1) * HEAD_DIM].astype(o_ref.dtype)
    else:
        o_ref[...] = acc.astype(o_ref.dtype)


def _rms_matmul(x, g_all, layer_g, w, *, tm, tn, out_dtype, gelu_cols=0, head_major=False,
                name):
    m, d = x.shape
    n = w.shape[1]
    assert m % tm == 0 and n % tn == 0
    if head_major:
        out_spec = pl.BlockSpec((tn // HEAD_DIM, tm, HEAD_DIM), lambda i, j: (j, i, 0))
        out_shape = jax.ShapeDtypeStruct((n // HEAD_DIM, m, HEAD_DIM), out_dtype)
    else:
        out_spec = pl.BlockSpec((tm, tn), lambda i, j: (i, j))
        out_shape = jax.ShapeDtypeStruct((m, n), out_dtype)
    return pl.pallas_call(
        functools.partial(_rms_matmul_kernel, gelu_cols=gelu_cols, head_major=head_major),
        grid=(m // tm, n // tn),
        in_specs=[
            pl.BlockSpec((tm, d), lambda i, j: (i, 0)),
            _layer_spec((1, d), lambda i, j: (0, 0), layer_g),
            pl.BlockSpec((d, tn), lambda i, j: (0, j)),
        ],
        out_specs=out_spec,
        out_shape=out_shape,
        scratch_shapes=[pltpu.VMEM((tm, d), jnp.bfloat16)],
        compiler_params=_params(2),
        name=name,
    )(x, g_all, w)


def _mem_kv_kernel(x_ref, g_ref, w_ref, o_ref):
    h = _rms(x_ref[...], g_ref[...]).astype(jnp.bfloat16)
    w = w_ref[...].astype(jnp.bfloat16)
    o_ref[...] = jnp.dot(h, w, preferred_element_type=jnp.float32).astype(o_ref.dtype)


def _mem_kv(mems, g_all, w_all):
    m, d = mems.shape
    layers, _, n = w_all.shape
    return pl.pallas_call(
        _mem_kv_kernel,
        grid=(layers,),
        in_specs=[
            pl.BlockSpec((m, d), lambda l: (0, 0)),
            pl.BlockSpec((None, 1, d), lambda l: (l, 0, 0)),
            pl.BlockSpec((None, d, n), lambda l: (l, 0, 0)),
        ],
        out_specs=pl.BlockSpec((None, m, n), lambda l: (l, 0, 0)),
        out_shape=jax.ShapeDtypeStruct((layers, m, n), jnp.bfloat16),
        compiler_params=_params(1),
        name="mem_kv",
    )(mems, g_all, w_all)


def _group_geometry(grp):
    window, dilation = A_PATTERNS[grp]
    n_side = (window // 2) // dilation
    length = SEQ // dilation
    kw = min(length, QBLK + 2 * n_side)
    return dilation, n_side, length, kw


def _key_start(qi, n_side, length, kw):
    return min(max(qi * QBLK - n_side, 0), length - kw)


def _dilated_attn_kernel(slopes_ref, *refs):
    qkv_refs = refs[:3 * A_GROUPS]
    tok_ref = refs[3 * A_GROUPS]
    o_scr, lse_scr, bias_scr = refs[3 * A_GROUPS + 1:3 * A_GROUPS + 4]
    stage = refs[3 * A_GROUPS + 4:]
    head = pl.program_id(1)
    scale = HEAD_DIM ** -0.5

    for grp in range(A_GROUPS):
        dilation, n_side, length, kw = _group_geometry(grp)
        nblk = length // QBLK
        q_ref, k_ref, v_ref = qkv_refs[3 * grp:3 * grp + 3]
        qs_ref, ks_ref, vs_ref = stage[3 * grp:3 * grp + 3]
        slope = slopes_ref[grp, head]

        for r in range(dilation):
            rows = pl.ds(r, length, stride=dilation) if dilation > 1 else slice(None)
            dst = slice(r * length, (r + 1) * length)
            qs_ref[dst, :] = q_ref[rows, :].astype(jnp.bfloat16)
            ks_ref[dst, :] = k_ref[rows, :].astype(jnp.bfloat16)
            vs_ref[dst, 0:HEAD_DIM] = v_ref[rows, :].astype(jnp.bfloat16)
        vs_ref[:, HEAD_DIM:2 * HEAD_DIM] = jnp.ones((SEQ, HEAD_DIM), jnp.bfloat16)

        offsets = sorted({qi * QBLK - _key_start(qi, n_side, length, kw) for qi in range(nblk)})
        for t, off in enumerate(offsets):
            rel = (lax.broadcasted_iota(jnp.int32, (QBLK, kw), 1)
                   - lax.broadcasted_iota(jnp.int32, (QBLK, kw), 0) - off)
            dist = jnp.abs(rel)
            alibi = (-slope) * (dist * dilation).astype(jnp.float32)
            bias_scr[grp, t, :, 0:kw] = jnp.where(dist <= n_side, alibi, NEG)

        for r in range(dilation):
            for qi in range(nblk):
                k0 = _key_start(qi, n_side, length, kw)
                t = offsets.index(qi * QBLK - k0)
                q = qs_ref[r * length + qi * QBLK:r * length + (qi + 1) * QBLK, :]
                k = ks_ref[r * length + k0:r * length + k0 + kw, :]
                v1 = vs_ref[r * length + k0:r * length + k0 + kw, :]
                bias = bias_scr[grp, t, :, 0:kw]
                s = lax.dot_general(q, k, (((1,), (1,)), ((), ())),
                                    preferred_element_type=jnp.float32) * scale
                s = jnp.where(bias > 0.5 * NEG, s + bias, NEG)
                m = jnp.max(s, axis=-1, keepdims=True)
                p = jnp.exp(s - m).astype(jnp.bfloat16)
                ol = jnp.dot(p, v1, preferred_element_type=jnp.float32)
                l = ol[:, HEAD_DIM:]
                start = qi * QBLK * dilation + r
                dst = pl.ds(start, QBLK, stride=dilation) if dilation > 1 else pl.ds(start, QBLK)
                o_scr[grp, dst, :] = ol[:, :HEAD_DIM] / l
                lse_scr[grp, dst, :] = m + jnp.log(l)

    rows_per_step = 256
    for c in range(SEQ // rows_per_step):
        rows = slice(c * rows_per_step, (c + 1) * rows_per_step)
        lses = [lse_scr[grp, rows, :] for grp in range(A_GROUPS)]
        mx = functools.reduce(jnp.maximum, lses)
        es = [jnp.exp(l - mx) for l in lses]
        den = functools.reduce(lambda a, b: a + b, es)
        tok = sum((e / den) * o_scr[grp, rows, :] for grp, e in enumerate(es))
        tok_ref[rows, :] = tok.astype(tok_ref.dtype)


def _dilated_attention(proj, *, batch):
    slopes = jnp.asarray(_alibi_slopes().reshape(A_GROUPS, HEADS_PER_GROUP))

    def col(which, grp):
        base = which * A_HEADS + grp * HEADS_PER_GROUP
        return lambda b, h, sl: (base + h, b, 0)

    in_specs = [pl.BlockSpec((None, SEQ, HEAD_DIM), col(which, grp))
                for grp in range(A_GROUPS) for which in range(3)]
    max_kw = max(_group_geometry(grp)[3] for grp in range(A_GROUPS))
    stage = []
    for _ in range(A_GROUPS):
        stage += [pltpu.VMEM((SEQ, HEAD_DIM), jnp.bfloat16),
                  pltpu.VMEM((SEQ, HEAD_DIM), jnp.bfloat16),
                  pltpu.VMEM((SEQ, 2 * HEAD_DIM), jnp.bfloat16)]
    return pl.pallas_call(
        _dilated_attn_kernel,
        grid_spec=pltpu.PrefetchScalarGridSpec(
            num_scalar_prefetch=1,
            grid=(batch, HEADS_PER_GROUP),
            in_specs=in_specs,
            out_specs=pl.BlockSpec((SEQ, HEAD_DIM), lambda b, h, sl: (b, h)),
            scratch_shapes=[
                pltpu.VMEM((A_GROUPS, SEQ, HEAD_DIM), jnp.float32),
                pltpu.VMEM((A_GROUPS, SEQ, HEAD_DIM), jnp.float32),
                pltpu.VMEM((A_GROUPS, 3, QBLK, max_kw), jnp.float32),
            ] + stage,
        ),
        out_shape=jax.ShapeDtypeStruct((batch * SEQ, GROUP_W), jnp.bfloat16),
        compiler_params=_params(2),
        name="dilated_attn",
    )(slopes, *([proj] * (3 * A_GROUPS)))


def _mem_attention(q_heads, kv_ref):
    scale = HEAD_DIM ** -0.5
    outs = []
    for h, q in enumerate(q_heads):
        k = kv_ref[:, h * HEAD_DIM:(h + 1) * HEAD_DIM]
        v = kv_ref[:, MEM_W + h * HEAD_DIM:MEM_W + (h + 1) * HEAD_DIM]
        s = lax.dot_general(q, k, (((1,), (1,)), ((), ())),
                            preferred_element_type=jnp.float32) * scale
        m = jnp.max(s, axis=-1, keepdims=True)
        p = jnp.exp(s - m)
        l = jnp.sum(p, axis=-1, keepdims=True)
        o = jnp.dot(p.astype(jnp.bfloat16), v, preferred_element_type=jnp.float32)
        outs.append(o / l)
    return jnp.concatenate(outs, axis=-1)


def _mix_out_a_kernel(x_ref, tok_ref, qm_ref, kv_ref, w_ref, out_ref):
    q_heads = [qm_ref[h].astype(jnp.bfloat16) for h in range(MEM_HEADS)]
    mem_out = _mem_attention(q_heads, kv_ref).astype(jnp.bfloat16)
    cat = jnp.concatenate([tok_ref[...], mem_out], axis=-1)
    out_ref[...] = x_ref[...] + jnp.dot(cat, w_ref[...], preferred_element_type=jnp.float32)


def _kv_spec(layer, batch, tiles_per_seq):
    return pl.BlockSpec((None, MEM_LEN, 2 * MEM_W),
                        lambda i: (layer * batch + i // tiles_per_seq, 0, 0))


def _mix_out_a(x, tok, proj, kv, w_out, layer, *, tm):
    m, d = x.shape
    tiles_per_seq = SEQ // tm
    batch = m // SEQ
    row = lambda i: (i, 0)
    return pl.pallas_call(
        _mix_out_a_kernel,
        grid=(m // tm,),
        in_specs=[
            pl.BlockSpec((tm, d), row),
            pl.BlockSpec((tm, GROUP_W), row),
            pl.BlockSpec((MEM_HEADS, tm, HEAD_DIM), lambda i: (3 * A_HEADS // MEM_HEADS, i, 0)),
            _kv_spec(layer, batch, tiles_per_seq),
            pl.BlockSpec(w_out.shape, lambda i: (0, 0)),
        ],
        out_specs=pl.BlockSpec((tm, d), row),
        out_shape=jax.ShapeDtypeStruct((m, d), jnp.float32),
        compiler_params=_params(1),
        name="mix_out_a",
    )(x, tok, proj, kv, w_out)


def _mix_out_b_kernel(x_ref, u_ref, v_ref, qm_ref, vg_ref, ws_ref, sb_ref, kv_ref, w_ref,
                      out_ref, tok_ref):
    tm = x_ref.shape[0]
    vn = _rms(v_ref[...], vg_ref[...]).astype(jnp.bfloat16)
    for c in range(tm // CHUNK):
        rows = slice(c * CHUNK, (c + 1) * CHUNK)
        for g in range(B_GROUPS):
            cols = slice(g * 128, (g + 1) * 128)
            mixed = jnp.dot(ws_ref[g], vn[rows, cols], preferred_element_type=jnp.float32)
            mixed = mixed + sb_ref[:, g:g + 1]
            tok_ref[rows, cols] = (u_ref[rows, cols] * mixed).astype(jnp.bfloat16)
    qm = qm_ref[...].astype(jnp.bfloat16)
    q_heads = [qm[:, h * HEAD_DIM:(h + 1) * HEAD_DIM] for h in range(MEM_HEADS)]
    mem_out = _mem_attention(q_heads, kv_ref).astype(jnp.bfloat16)
    cat = jnp.concatenate([tok_ref[...], mem_out], axis=-1)
    out_ref[...] = x_ref[...] + jnp.dot(cat, w_ref[...], preferred_element_type=jnp.float32)


def _mix_out_b(x, uvq, kv, v_norm_g_all, w_s_all, s_bias_t_all, w_out, layer, layer_b, *, tm):
    m, d = x.shape
    tiles_per_seq = SEQ // tm
    batch = m // SEQ
    row = lambda i: (i, 0)
    const2 = lambda i: (0, 0)
    return pl.pallas_call(
        _mix_out_b_kernel,
        grid=(m // tm,),
        in_specs=[
            pl.BlockSpec((tm, d), row),
            pl.BlockSpec((tm, B_W), row),
            pl.BlockSpec((tm, B_W), lambda i: (i, 1)),
            pl.BlockSpec((tm, MEM_W), lambda i: (i, 2 * B_W // MEM_W)),
            _layer_spec((1, B_W), const2, layer_b),
            _layer_spec(w_s_all.shape[1:], lambda i: (0, 0, 0), layer_b),
            _layer_spec(s_bias_t_all.shape[1:], const2, layer_b),
            _kv_spec(layer, batch, tiles_per_seq),
            pl.BlockSpec(w_out.shape, const2),
        ],
        out_specs=pl.BlockSpec((tm, d), row),
        out_shape=jax.ShapeDtypeStruct((m, d), jnp.float32),
        scratch_shapes=[pltpu.VMEM((tm, B_W), jnp.bfloat16)],
        compiler_params=_params(1),
        name="mix_out_b",
    )(x, uvq, uvq, uvq, v_norm_g_all, w_s_all, s_bias_t_all, kv, w_out)


BF16_ROWS = 16
HALO = BF16_ROWS
FFN_TM = 1024
FFN_TF = 512
MXU_N = 256


N_FFN_IN = 9
CONV_TAPS = 3


def _conv_ffn_kernel(*refs, n_casts, tiles_per_seq, final_norm):
    (x_hbm, xp_ref, xn_ref, g_ref, wg_ref, wv_ref, cp_ref, wd_ref,
     fg_ref) = refs[:N_FFN_IN]
    cast_in = refs[N_FFN_IN:N_FFN_IN + n_casts]
    out_ref = refs[N_FFN_IN + n_casts]
    cast_out = refs[N_FFN_IN + n_casts + 1:N_FFN_IN + 2 * n_casts + 1]
    h_ref, ag_ref, av_ref, x_ref, x_sem = refs[N_FFN_IN + 2 * n_casts + 1:]
    i = pl.program_id(0)
    j = pl.program_id(1)
    n_tiles = pl.num_programs(0)
    nj = pl.num_programs(1)
    tm = x_ref.shape[0]

    def x_copy(tile):
        return pltpu.make_async_copy(x_hbm.at[pl.ds(tile * tm, tm), :], x_ref, x_sem)

    @pl.when(jnp.logical_and(i == 0, j == 0))
    def _():
        x_copy(0).start()

    @pl.when(j == 0)
    def _():
        x_copy(i).wait()
        g = g_ref[...]
        _rms_to(h_ref, x_ref, g, copy_ref=out_ref)
        first = (i % tiles_per_seq) == 0
        last = (i % tiles_per_seq) == tiles_per_seq - 1
        r = lax.broadcasted_iota(jnp.int32, (HALO, 1), 0)
        take_next = jnp.logical_and(r == 0, jnp.logical_not(last))
        take_prev = jnp.logical_and(r == HALO - 1, jnp.logical_not(first))
        halo = jnp.where(take_next, _rms(xn_ref[...], g),
                         jnp.where(take_prev, _rms(xp_ref[...], g), 0.0))
        h_ref[tm:tm + HALO, :] = halo.astype(jnp.bfloat16)

    @pl.when(jnp.logical_and(j == 1, i + 1 < n_tiles))
    def _():
        x_copy(i + 1).start()

    for src, dst in zip(cast_in, cast_out):
        dst[...] = src[...].astype(jnp.bfloat16)
    h = h_ref[...]

    def up(w_ref, a_ref):
        a = jnp.dot(h, w_ref[...], preferred_element_type=jnp.float32)
        a_ref[HALO:HALO + tm, :] = a[0:tm]
        a_ref[0:HALO, :] = a[tm:tm + HALO]
        a_ref[HALO + tm:2 * HALO + tm, :] = a[tm:tm + HALO]

    def conv(a_ref, half, cols):
        cp = cp_ref[half]
        return (a_ref[HALO - 1:HALO - 1 + tm, cols] * cp[0:1, cols]
                + a_ref[HALO:HALO + tm, cols] * cp[1:2, cols]
                + a_ref[HALO + 1:HALO + 1 + tm, cols] * cp[2:3, cols]
                + cp[CONV_TAPS:CONV_TAPS + 1, cols])

    up(wg_ref, ag_ref)
    up(wv_ref, av_ref)
    tf = wg_ref.shape[1]
    upd = None
    for c in range(tf // MXU_N):
        cols = slice(c * MXU_N, (c + 1) * MXU_N)
        act = (_gelu(conv(ag_ref, 0, cols)) * conv(av_ref, 1, cols)).astype(jnp.bfloat16)
        part = jnp.dot(act, wd_ref[cols, :], preferred_element_type=jnp.float32)
        upd = part if upd is None else upd + part
    out_ref[...] += upd

    if final_norm:
        @pl.when(j == nj - 1)
        def _():
            out_ref[...] = _rms(out_ref[...], fg_ref[...])


def _conv_ffn(x, g_all, w_up, w_down, cp_all, final_g, layer, casts, *, tm, tf, final_norm):
    m, d = x.shape
    nf = FF // tf
    assert m % tm == 0 and FF % tf == 0 and SEQ % tm == 0 and tm % HALO == 0
    assert nf >= 2
    n_tiles = m // tm
    n_steps = n_tiles * nf
    tiles_per_seq = SEQ // tm
    hb = tm // HALO
    n_hblocks = m // HALO

    cast_in_specs, cast_out_specs, cast_out_shapes = [], [], []
    for src, src_layer in casts:
        _, rows, width = src.shape
        block_rows = BF16_ROWS * pl.cdiv(rows, BF16_ROWS * n_steps)
        n_blocks = rows // block_rows
        assert rows % block_rows == 0 and n_blocks <= n_steps

        def imap(i, j, n_blocks=n_blocks):
            return (jnp.minimum(i * nf + j, n_blocks - 1), 0)

        cast_in_specs.append(_layer_spec((block_rows, width), imap, src_layer))
        cast_out_specs.append(pl.BlockSpec((block_rows, width), imap))
        cast_out_shapes.append(jax.ShapeDtypeStruct((rows, width), jnp.bfloat16))

    outs = pl.pallas_call(
        functools.partial(_conv_ffn_kernel, n_casts=len(casts),
                          tiles_per_seq=tiles_per_seq, final_norm=final_norm),
        grid=(n_tiles, nf),
        in_specs=[
            pl.BlockSpec(memory_space=pl.ANY),
            pl.BlockSpec((HALO, d), lambda i, j: (jnp.maximum(i * hb - 1, 0), 0)),
            pl.BlockSpec((HALO, d), lambda i, j: (jnp.minimum((i + 1) * hb, n_hblocks - 1), 0)),
            _layer_spec((1, d), lambda i, j: (0, 0), layer),
            pl.BlockSpec((d, tf), lambda i, j: (0, j)),
            pl.BlockSpec((d, tf), lambda i, j: (0, nf + j)),
            _layer_spec((2, CONV_TAPS + 1, tf), lambda i, j: (0, 0, j), layer),
            pl.BlockSpec((tf, d), lambda i, j: (j, 0)),
            pl.BlockSpec((1, d), lambda i, j: (0, 0)),
        ] + cast_in_specs,
        out_specs=[pl.BlockSpec((tm, d), lambda i, j: (i, 0))] + cast_out_specs,
        out_shape=[jax.ShapeDtypeStruct((m, d), jnp.float32)] + cast_out_shapes,
        scratch_shapes=[
            pltpu.VMEM((tm + HALO, d), jnp.bfloat16),
            pltpu.VMEM((tm + 2 * HALO, tf), jnp.float32),
            pltpu.VMEM((tm + 2 * HALO, tf), jnp.float32),
            pltpu.VMEM((tm, d), jnp.float32),
            pltpu.SemaphoreType.DMA(()),
        ],
        compiler_params=_params(2),
        name="conv_ffn",
    )(x, x, x, g_all, w_up, w_up, cp_all, w_down, final_g, *[c[0] for c in casts])
    return outs[0], outs[1:]


def kernel(x, mem, mix_norm_g, ffn_norm_g, mem_norm_g, w_mem_kv, a_w_in, a_w_out, b_w_in,
           b_v_norm_g, b_w_s, b_s_bias, b_w_out, ffn_w_up, ffn_conv_w, ffn_conv_b, ffn_w_down,
           final_norm_g):
    batch, seq, d = x.shape
    assert (seq, d) == (SEQ, D_MODEL)
    bf = jnp.bfloat16
    xs = x.reshape(batch * seq, d)
    mems = mem.reshape(batch * MEM_LEN, d)

    mix_g = mix_norm_g.reshape(DEPTH, 1, d)
    ffn_g = ffn_norm_g.reshape(DEPTH, 1, d)
    mem_g = mem_norm_g.reshape(DEPTH, 1, d)
    final_g = final_norm_g.reshape(1, d)
    v_norm_g = b_v_norm_g.reshape(-1, 1, B_W)
    s_bias_t = jnp.swapaxes(b_s_bias, 1, 2)
    conv_p = jnp.concatenate([ffn_conv_w.reshape(DEPTH, CONV_TAPS, 2, FF).transpose(0, 2, 1, 3),
                              ffn_conv_b.reshape(DEPTH, 2, 1, FF)], axis=2)
    b_w_s = b_w_s.astype(bf)
    kv = _mem_kv(mems, mem_g, w_mem_kv).reshape(DEPTH * batch, MEM_LEN, 2 * MEM_W)

    def mixer_weights(i):
        return (a_w_in, a_w_out) if i % 2 == 0 else (b_w_in, b_w_out)

    w_in, w_out = (w[0].astype(bf) for w in mixer_weights(0))
    w_up, w_down = ffn_w_up[0].astype(bf), ffn_w_down[0].astype(bf)

    for i in range(DEPTH):
        j = i // 2
        if i % 2 == 0:
            proj = _rms_matmul(xs, mix_g, i, w_in, tm=1024, tn=1280, out_dtype=jnp.float32,
                               head_major=True, name="in_proj_a")
            tok = _dilated_attention(proj, batch=batch)
            xs = _mix_out_a(xs, tok, proj, kv, w_out, i, tm=512)
        else:
            uvq = _rms_matmul(xs, mix_g, i, w_in, tm=1024, tn=1792, out_dtype=jnp.float32,
                              gelu_cols=2 * B_W, name="in_proj_b")
            xs = _mix_out_b(xs, uvq, kv, v_norm_g, b_w_s, s_bias_t, w_out, i, j, tm=512)
        casts = []
        if i + 1 < DEPTH:
            nxt_in, nxt_out = mixer_weights(i + 1)
            casts = [(nxt_in, (i + 1) // 2), (nxt_out, (i + 1) // 2), (ffn_w_up, i + 1),
                     (ffn_w_down, i + 1)]
        xs, nxt = _conv_ffn(xs, ffn_g, w_up, w_down, conv_p, final_g, i, casts,
                            tm=FFN_TM, tf=FFN_TF, final_norm=(i == DEPTH - 1))
        if nxt:
            w_in, w_out, w_up, w_down = nxt
    return xs.reshape(batch, seq, d)
```

```python
import functools

import numpy as np
import jax
import jax.numpy as jnp
from jax import lax
from jax.experimental import pallas as pl
from jax.experimental.pallas import tpu as pltpu

D_MODEL = 2048
SEQ = 2048
DEPTH = 4
EPS = 1e-6
NEG = -1e30

HEAD_DIM = 128
HEADS_PER_GROUP = 4
A_PATTERNS = ((128, 1), (512, 4), (2048, 16))
A_GROUPS = len(A_PATTERNS)
A_HEADS = HEADS_PER_GROUP * A_GROUPS
A_QKV_W = A_HEADS * HEAD_DIM
GROUP_W = HEADS_PER_GROUP * HEAD_DIM
QBLK = 128

CHUNK = 128
B_GROUPS = 12
B_W = B_GROUPS * 128

MEM_LEN = 256
MEM_HEADS = 4
MEM_W = MEM_HEADS * HEAD_DIM

A_IN = 3 * A_QKV_W + MEM_W
B_IN = 2 * B_W + MEM_W
FF = 5632

VMEM_LIMIT_BYTES = 62 * 1024 * 1024

_SQRT_HALF = 0.7071067811865476


def _params(n_axes):
    return pltpu.CompilerParams(
        dimension_semantics=("arbitrary",) * n_axes,
        vmem_limit_bytes=VMEM_LIMIT_BYTES,
    )


def _rms(x, g):
    y = x * lax.rsqrt(jnp.mean(x * x, axis=-1, keepdims=True) + EPS)
    return y * g


NORM_ROWS = 16


def _rms_to(dst_ref, x_ref, g, copy_ref=None):
    n = x_ref.shape[0]
    for r0 in range(0, n, NORM_ROWS):
        rows = slice(r0, r0 + NORM_ROWS)
        x = x_ref[rows, :]
        dst_ref[rows, :] = _rms(x, g).astype(dst_ref.dtype)
        if copy_ref is not None:
            copy_ref[rows, :] = x


def _gelu(x):
    return 0.5 * x * (1.0 + lax.erf(x * _SQRT_HALF))


def _alibi_slopes():
    return (2.0 ** (-8.0 * (np.arange(A_HEADS) + 1) / A_HEADS)).astype(np.float32)


def _layer_spec(block, index_map, layer):
    return pl.BlockSpec((None,) + block, lambda *g: (layer,) + index_map(*g))


def _resident(shape):
    return pl.BlockSpec(shape, lambda *g: (0,) * len(shape), pipeline_mode=pl.Buffered(1))


def _in_proj_a_kernel(x_ref, g_ref, w_ref, o_ref, h_ref, *, tn):
    _rms_to(h_ref, x_ref, g_ref[...])
    h = h_ref[...]
    for c in range(w_ref.shape[1] // tn):
        acc = jnp.dot(h, w_ref[:, c * tn:(c + 1) * tn], preferred_element_type=jnp.float32)
        for cc in range(tn // HEAD_DIM):
            o_ref[c * (tn // HEAD_DIM) + cc] = acc[:, cc * HEAD_DIM:(cc + 1) * HEAD_DIM]


def _in_proj_a(x, g_all, layer_g, w, *, tm, tn):
    m, d = x.shape
    n = w.shape[1]
    assert m % tm == 0 and n % tn == 0 and tn % HEAD_DIM == 0
    return pl.pallas_call(
        functools.partial(_in_proj_a_kernel, tn=tn),
        grid=(m // tm,),
        in_specs=[
            pl.BlockSpec((tm, d), lambda i: (i, 0)),
            _layer_spec((1, d), lambda i: (0, 0), layer_g),
            _resident(w.shape),
        ],
        out_specs=pl.BlockSpec((n // HEAD_DIM, tm, HEAD_DIM), lambda i: (0, i, 0)),
        out_shape=jax.ShapeDtypeStruct((n // HEAD_DIM, m, HEAD_DIM), jnp.float32),
        scratch_shapes=[pltpu.VMEM((tm, d), jnp.bfloat16)],
        compiler_params=_params(1),
        name="in_proj_a",
    )(x, g_all, w)


def _mem_kv_kernel(x_ref, g_ref, w_ref, o_ref):
    h = _rms(x_ref[...], g_ref[...]).astype(jnp.bfloat16)
    w = w_ref[...].astype(jnp.bfloat16)
    o_ref[...] = jnp.dot(h, w, preferred_element_type=jnp.float32).astype(o_ref.dtype)


def _mem_kv(mems, g_all, w_all):
    m, d = mems.shape
    layers, _, n = w_all.shape
    return pl.pallas_call(
        _mem_kv_kernel,
        grid=(layers,),
        in_specs=[
            pl.BlockSpec((m, d), lambda l: (0, 0)),
            pl.BlockSpec((None, 1, d), lambda l: (l, 0, 0)),
            pl.BlockSpec((None, d, n), lambda l: (l, 0, 0)),
        ],
        out_specs=pl.BlockSpec((None, m, n), lambda l: (l, 0, 0)),
        out_shape=jax.ShapeDtypeStruct((layers, m, n), jnp.bfloat16),
        compiler_params=_params(1),
        name="mem_kv",
    )(mems, g_all, w_all)


def _group_geometry(grp):
    window, dilation = A_PATTERNS[grp]
    n_side = (window // 2) // dilation
    length = SEQ // dilation
    kw = min(length, QBLK + 2 * n_side)
    return dilation, n_side, length, kw


def _key_start(qi, n_side, length, kw):
    return min(max(qi * QBLK - n_side, 0), length - kw)


def _dilated_attn_kernel(slopes_ref, *refs):
    qkv_refs = refs[:3 * A_GROUPS]
    tok_ref = refs[3 * A_GROUPS]
    o_scr, lse_scr, bias_scr = refs[3 * A_GROUPS + 1:3 * A_GROUPS + 4]
    stage = refs[3 * A_GROUPS + 4:]
    head = pl.program_id(1)
    scale = HEAD_DIM ** -0.5

    for grp in range(A_GROUPS):
        dilation, n_side, length, kw = _group_geometry(grp)
        nblk = length // QBLK
        q_ref, k_ref, v_ref = qkv_refs[3 * grp:3 * grp + 3]
        qs_ref, ks_ref, vs_ref = stage[3 * grp:3 * grp + 3]
        slope = slopes_ref[grp, head]

        for r in range(dilation):
            rows = pl.ds(r, length, stride=dilation) if dilation > 1 else slice(None)
            dst = slice(r * length, (r + 1) * length)
            qs_ref[dst, :] = q_ref[rows, :].astype(jnp.bfloat16)
            ks_ref[dst, :] = k_ref[rows, :].astype(jnp.bfloat16)
            vs_ref[dst, 0:HEAD_DIM] = v_ref[rows, :].astype(jnp.bfloat16)
        vs_ref[:, HEAD_DIM:2 * HEAD_DIM] = jnp.ones((SEQ, HEAD_DIM), jnp.bfloat16)

        offsets = sorted({qi * QBLK - _key_start(qi, n_side, length, kw) for qi in range(nblk)})
        for t, off in enumerate(offsets):
            rel = (lax.broadcasted_iota(jnp.int32, (QBLK, kw), 1)
                   - lax.broadcasted_iota(jnp.int32, (QBLK, kw), 0) - off)
            dist = jnp.abs(rel)
            alibi = (-slope) * (dist * dilation).astype(jnp.float32)
            bias_scr[grp, t, :, 0:kw] = jnp.where(dist <= n_side, alibi, NEG)

        for r in range(dilation):
            for qi in range(nblk):
                k0 = _key_start(qi, n_side, length, kw)
                t = offsets.index(qi * QBLK - k0)
                q = qs_ref[r * length + qi * QBLK:r * length + (qi + 1) * QBLK, :]
                k = ks_ref[r * length + k0:r * length + k0 + kw, :]
                v1 = vs_ref[r * length + k0:r * length + k0 + kw, :]
                bias = bias_scr[grp, t, :, 0:kw]
                s = lax.dot_general(q, k, (((1,), (1,)), ((), ())),
                                    preferred_element_type=jnp.float32) * scale
                s = jnp.where(bias > 0.5 * NEG, s + bias, NEG)
                m = jnp.max(s, axis=-1, keepdims=True)
                p = jnp.exp(s - m).astype(jnp.bfloat16)
                ol = jnp.dot(p, v1, preferred_element_type=jnp.float32)
                l = ol[:, HEAD_DIM:]
                start = qi * QBLK * dilation + r
                dst = pl.ds(start, QBLK, stride=dilation) if dilation > 1 else pl.ds(start, QBLK)
                o_scr[grp, dst, :] = ol[:, :HEAD_DIM] / l
                lse_scr[grp, dst, :] = m + jnp.log(l)

    rows_per_step = 256
    for c in range(SEQ // rows_per_step):
        rows = slice(c * rows_per_step, (c + 1) * rows_per_step)
        lses = [lse_scr[grp, rows, :] for grp in range(A_GROUPS)]
        mx = functools.reduce(jnp.maximum, lses)
        es = [jnp.exp(l - mx) for l in lses]
        den = functools.reduce(lambda a, b: a + b, es)
        tok = sum((e / den) * o_scr[grp, rows, :] for grp, e in enumerate(es))
        tok_ref[rows, :] = tok.astype(tok_ref.dtype)


def _dilated_attention(proj, *, batch):
    slopes = jnp.asarray(_alibi_slopes().reshape(A_GROUPS, HEADS_PER_GROUP))

    def col(which, grp):
        base = which * A_HEADS + grp * HEADS_PER_GROUP
        return lambda b, h, sl: (base + h, b, 0)

    in_specs = [pl.BlockSpec((None, SEQ, HEAD_DIM), col(which, grp))
                for grp in range(A_GROUPS) for which in range(3)]
    max_kw = max(_group_geometry(grp)[3] for grp in range(A_GROUPS))
    stage = []
    for _ in range(A_GROUPS):
        stage += [pltpu.VMEM((SEQ, HEAD_DIM), jnp.bfloat16),
                  pltpu.VMEM((SEQ, HEAD_DIM), jnp.bfloat16),
                  pltpu.VMEM((SEQ, 2 * HEAD_DIM), jnp.bfloat16)]
    return pl.pallas_call(
        _dilated_attn_kernel,
        grid_spec=pltpu.PrefetchScalarGridSpec(
            num_scalar_prefetch=1,
            grid=(batch, HEADS_PER_GROUP),
            in_specs=in_specs,
            out_specs=pl.BlockSpec((SEQ, HEAD_DIM), lambda b, h, sl: (b, h)),
            scratch_shapes=[
                pltpu.VMEM((A_GROUPS, SEQ, HEAD_DIM), jnp.float32),
                pltpu.VMEM((A_GROUPS, SEQ, HEAD_DIM), jnp.float32),
                pltpu.VMEM((A_GROUPS, 3, QBLK, max_kw), jnp.float32),
            ] + stage,
        ),
        out_shape=jax.ShapeDtypeStruct((batch * SEQ, GROUP_W), jnp.bfloat16),
        compiler_params=_params(2),
        name="dilated_attn",
    )(slopes, *([proj] * (3 * A_GROUPS)))


def _mem_attention(q_heads, kv_ref):
    scale = HEAD_DIM ** -0.5
    outs = []
    for h, q in enumerate(q_heads):
        k = kv_ref[:, h * HEAD_DIM:(h + 1) * HEAD_DIM]
        v = kv_ref[:, MEM_W + h * HEAD_DIM:MEM_W + (h + 1) * HEAD_DIM]
        s = lax.dot_general(q, k, (((1,), (1,)), ((), ())),
                            preferred_element_type=jnp.float32) * scale
        m = jnp.max(s, axis=-1, keepdims=True)
        p = jnp.exp(s - m)
        l = jnp.sum(p, axis=-1, keepdims=True)
        o = jnp.dot(p.astype(jnp.bfloat16), v, preferred_element_type=jnp.float32)
        outs.append(o / l)
    return jnp.concatenate(outs, axis=-1)


def _mix_out_a_kernel(x_ref, tok_ref, qm_ref, kv_ref, w_ref, out_ref):
    q_heads = [qm_ref[h].astype(jnp.bfloat16) for h in range(MEM_HEADS)]
    mem_out = _mem_attention(q_heads, kv_ref).astype(jnp.bfloat16)
    cat = jnp.concatenate([tok_ref[...], mem_out], axis=-1)
    out_ref[...] = x_ref[...] + jnp.dot(cat, w_ref[...], preferred_element_type=jnp.float32)


def _kv_spec(layer, batch, tiles_per_seq):
    return pl.BlockSpec((None, MEM_LEN, 2 * MEM_W),
                        lambda i: (layer * batch + i // tiles_per_seq, 0, 0))


def _mix_out_a(x, tok, proj, kv, w_out, layer, *, tm):
    m, d = x.shape
    tiles_per_seq = SEQ // tm
    batch = m // SEQ
    row = lambda i: (i, 0)
    return pl.pallas_call(
        _mix_out_a_kernel,
        grid=(m // tm,),
        in_specs=[
            pl.BlockSpec((tm, d), row),
            pl.BlockSpec((tm, GROUP_W), row),
            pl.BlockSpec((MEM_HEADS, tm, HEAD_DIM), lambda i: (3 * A_HEADS // MEM_HEADS, i, 0)),
            _kv_spec(layer, batch, tiles_per_seq),
            pl.BlockSpec(w_out.shape, lambda i: (0, 0)),
        ],
        out_specs=pl.BlockSpec((tm, d), row),
        out_shape=jax.ShapeDtypeStruct((m, d), jnp.float32),
        compiler_params=_params(1),
        name="mix_out_a",
    )(x, tok, proj, kv, w_out)


def _mixer_b_kernel(x_ref, g_ref, win_ref, vg_ref, ws_ref, sb_ref, kv_ref, wout_ref, out_ref,
                    h_ref, uvq_ref, tok_ref, *, tn):
    tm = x_ref.shape[0]
    _rms_to(h_ref, x_ref, g_ref[...])
    h = h_ref[...]
    for c in range(B_IN // tn):
        acc = jnp.dot(h, win_ref[:, c * tn:(c + 1) * tn], preferred_element_type=jnp.float32)
        n_gelu = min(max(2 * B_W - c * tn, 0), tn)
        if n_gelu:
            uvq_ref[:, c * tn:c * tn + n_gelu] = _gelu(acc[:, :n_gelu])
        if n_gelu < tn:
            uvq_ref[:, c * tn + n_gelu:(c + 1) * tn] = acc[:, n_gelu:]

    vn = _rms(uvq_ref[:, B_W:2 * B_W], vg_ref[...]).astype(jnp.bfloat16)
    for c in range(tm // CHUNK):
        rows = slice(c * CHUNK, (c + 1) * CHUNK)
        for g in range(B_GROUPS):
            cols = slice(g * 128, (g + 1) * 128)
            mixed = jnp.dot(ws_ref[g], vn[rows, cols], preferred_element_type=jnp.float32)
            mixed = mixed + sb_ref[:, g:g + 1]
            tok_ref[rows, cols] = (uvq_ref[rows, cols] * mixed).astype(jnp.bfloat16)
    qm = uvq_ref[:, 2 * B_W:].astype(jnp.bfloat16)
    q_heads = [qm[:, hd * HEAD_DIM:(hd + 1) * HEAD_DIM] for hd in range(MEM_HEADS)]
    mem_out = _mem_attention(q_heads, kv_ref).astype(jnp.bfloat16)
    cat = jnp.concatenate([tok_ref[...], mem_out], axis=-1)
    out_ref[...] = x_ref[...] + jnp.dot(cat, wout_ref[...], preferred_element_type=jnp.float32)


def _mixer_b(x, g_all, w_in, kv, v_norm_g_all, w_s_all, s_bias_t_all, w_out, layer, layer_b, *,
             tm, tn):
    m, d = x.shape
    assert m % tm == 0 and SEQ % tm == 0 and tm % CHUNK == 0 and B_IN % tn == 0
    tiles_per_seq = SEQ // tm
    batch = m // SEQ
    row = lambda i: (i, 0)
    const2 = lambda i: (0, 0)
    return pl.pallas_call(
        functools.partial(_mixer_b_kernel, tn=tn),
        grid=(m // tm,),
        in_specs=[
            pl.BlockSpec((tm, d), row),
            _layer_spec((1, d), const2, layer),
            _resident(w_in.shape),
            _layer_spec((1, B_W), const2, layer_b),
            _layer_spec(w_s_all.shape[1:], lambda i: (0, 0, 0), layer_b),
            _layer_spec(s_bias_t_all.shape[1:], const2, layer_b),
            _kv_spec(layer, batch, tiles_per_seq),
            _resident(w_out.shape),
        ],
        out_specs=pl.BlockSpec((tm, d), row),
        out_shape=jax.ShapeDtypeStruct((m, d), jnp.float32),
        scratch_shapes=[
            pltpu.VMEM((tm, d), jnp.bfloat16),
            pltpu.VMEM((tm, B_IN), jnp.float32),
            pltpu.VMEM((tm, B_W), jnp.bfloat16),
        ],
        compiler_params=_params(1),
        name="mixer_b",
    )(x, g_all, w_in, v_norm_g_all, w_s_all, s_bias_t_all, kv, w_out)


BF16_ROWS = 16
HALO = BF16_ROWS
FFN_TM = 1024
FFN_TF = 512
MXU_N = 256


N_FFN_IN = 9
CONV_TAPS = 3


def _conv_ffn_kernel(*refs, n_casts, tiles_per_seq, final_norm):
    (x_hbm, xp_ref, xn_ref, g_ref, wg_ref, wv_ref, cp_ref, wd_ref,
     fg_ref) = refs[:N_FFN_IN]
    cast_in = refs[N_FFN_IN:N_FFN_IN + n_casts]
    out_ref = refs[N_FFN_IN + n_casts]
    cast_out = refs[N_FFN_IN + n_casts + 1:N_FFN_IN + 2 * n_casts + 1]
    h_ref, ag_ref, av_ref, x_ref, x_sem = refs[N_FFN_IN + 2 * n_casts + 1:]
    i = pl.program_id(0)
    j = pl.program_id(1)
    n_tiles = pl.num_programs(0)
    nj = pl.num_programs(1)
    tm = x_ref.shape[0]

    def x_copy(tile):
        return pltpu.make_async_copy(x_hbm.at[pl.ds(tile * tm, tm), :], x_ref, x_sem)

    @pl.when(jnp.logical_and(i == 0, j == 0))
    def _():
        x_copy(0).start()

    @pl.when(j == 0)
    def _():
        x_copy(i).wait()
        g = g_ref[...]
        _rms_to(h_ref, x_ref, g, copy_ref=out_ref)
        first = (i % tiles_per_seq) == 0
        last = (i % tiles_per_seq) == tiles_per_seq - 1
        r = lax.broadcasted_iota(jnp.int32, (HALO, 1), 0)
        take_next = jnp.logical_and(r == 0, jnp.logical_not(last))
        take_prev = jnp.logical_and(r == HALO - 1, jnp.logical_not(first))
        halo = jnp.where(take_next, _rms(xn_ref[...], g),
                         jnp.where(take_prev, _rms(xp_ref[...], g), 0.0))
        h_ref[tm:tm + HALO, :] = halo.astype(jnp.bfloat16)

    @pl.when(jnp.logical_and(j == 1, i + 1 < n_tiles))
    def _():
        x_copy(i + 1).start()

    for src, dst in zip(cast_in, cast_out):
        dst[...] = src[...].astype(jnp.bfloat16)
    h = h_ref[...]

    def up(w_ref, a_ref):
        a = jnp.dot(h, w_ref[...], preferred_element_type=jnp.float32)
        a_ref[HALO:HALO + tm, :] = a[0:tm]
        a_ref[0:HALO, :] = a[tm:tm + HALO]
        a_ref[HALO + tm:2 * HALO + tm, :] = a[tm:tm + HALO]

    def conv(a_ref, half, cols):
        cp = cp_ref[half]
        return (a_ref[HALO - 1:HALO - 1 + tm, cols] * cp[0:1, cols]
                + a_ref[HALO:HALO + tm, cols] * cp[1:2, cols]
                + a_ref[HALO + 1:HALO + 1 + tm, cols] * cp[2:3, cols]
                + cp[CONV_TAPS:CONV_TAPS + 1, cols])

    up(wg_ref, ag_ref)
    up(wv_ref, av_ref)
    tf = wg_ref.shape[1]
    upd = None
    for c in range(tf // MXU_N):
        cols = slice(c * MXU_N, (c + 1) * MXU_N)
        act = (_gelu(conv(ag_ref, 0, cols)) * conv(av_ref, 1, cols)).astype(jnp.bfloat16)
        part = jnp.dot(act, wd_ref[cols, :], preferred_element_type=jnp.float32)
        upd = part if upd is None else upd + part
    out_ref[...] += upd

    if final_norm:
        @pl.when(j == nj - 1)
        def _():
            out_ref[...] = _rms(out_ref[...], fg_ref[...])


def _conv_ffn(x, g_all, w_up, w_down, cp_all, final_g, layer, casts, *, tm, tf, final_norm):
    m, d = x.shape
    nf = FF // tf
    assert m % tm == 0 and FF % tf == 0 and SEQ % tm == 0 and tm % HALO == 0
    assert nf >= 2
    n_tiles = m // tm
    n_steps = n_tiles * nf
    tiles_per_seq = SEQ // tm
    hb = tm // HALO
    n_hblocks = m // HALO

    cast_in_specs, cast_out_specs, cast_out_shapes = [], [], []
    for src, src_layer in casts:
        _, rows, width = src.shape
        block_rows = BF16_ROWS * pl.cdiv(rows, BF16_ROWS * n_steps)
        n_blocks = rows // block_rows
        assert rows % block_rows == 0 and n_blocks <= n_steps

        def imap(i, j, n_blocks=n_blocks):
            return (jnp.minimum(i * nf + j, n_blocks - 1), 0)

        cast_in_specs.append(_layer_spec((block_rows, width), imap, src_layer))
        cast_out_specs.append(pl.BlockSpec((block_rows, width), imap))
        cast_out_shapes.append(jax.ShapeDtypeStruct((rows, width), jnp.bfloat16))

    outs = pl.pallas_call(
        functools.partial(_conv_ffn_kernel, n_casts=len(casts),
                          tiles_per_seq=tiles_per_seq, final_norm=final_norm),
        grid=(n_tiles, nf),
        in_specs=[
            pl.BlockSpec(memory_space=pl.ANY),
            pl.BlockSpec((HALO, d), lambda i, j: (jnp.maximum(i * hb - 1, 0), 0)),
            pl.BlockSpec((HALO, d), lambda i, j: (jnp.minimum((i + 1) * hb, n_hblocks - 1), 0)),
            _layer_spec((1, d), lambda i, j: (0, 0), layer),
            pl.BlockSpec((d, tf), lambda i, j: (0, j)),
            pl.BlockSpec((d, tf), lambda i, j: (0, nf + j)),
            _layer_spec((2, CONV_TAPS + 1, tf), lambda i, j: (0, 0, j), layer),
            pl.BlockSpec((tf, d), lambda i, j: (j, 0)),
            pl.BlockSpec((1, d), lambda i, j: (0, 0)),
        ] + cast_in_specs,
        out_specs=[pl.BlockSpec((tm, d), lambda i, j: (i, 0))] + cast_out_specs,
        out_shape=[jax.ShapeDtypeStruct((m, d), jnp.float32)] + cast_out_shapes,
        scratch_shapes=[
            pltpu.VMEM((tm + HALO, d), jnp.bfloat16),
            pltpu.VMEM((tm + 2 * HALO, tf), jnp.float32),
            pltpu.VMEM((tm + 2 * HALO, tf), jnp.float32),
            pltpu.VMEM((tm, d), jnp.float32),
            pltpu.SemaphoreType.DMA(()),
        ],
        compiler_params=_params(2),
        name="conv_ffn",
    )(x, x, x, g_all, w_up, w_up, cp_all, w_down, final_g, *[c[0] for c in casts])
    return outs[0], outs[1:]


def kernel(x, mem, mix_norm_g, ffn_norm_g, mem_norm_g, w_mem_kv, a_w_in, a_w_out, b_w_in,
           b_v_norm_g, b_w_s, b_s_bias, b_w_out, ffn_w_up, ffn_conv_w, ffn_conv_b, ffn_w_down,
           final_norm_g):
    batch, seq, d = x.shape
    assert (seq, d) == (SEQ, D_MODEL)
    bf = jnp.bfloat16
    xs = x.reshape(batch * seq, d)
    mems = mem.reshape(batch * MEM_LEN, d)

    mix_g = mix_norm_g.reshape(DEPTH, 1, d)
    ffn_g = ffn_norm_g.reshape(DEPTH, 1, d)
    mem_g = mem_norm_g.reshape(DEPTH, 1, d)
    final_g = final_norm_g.reshape(1, d)
    v_norm_g = b_v_norm_g.reshape(-1, 1, B_W)
    s_bias_t = jnp.swapaxes(b_s_bias, 1, 2)
    conv_p = jnp.concatenate([ffn_conv_w.reshape(DEPTH, CONV_TAPS, 2, FF).transpose(0, 2, 1, 3),
                              ffn_conv_b.reshape(DEPTH, 2, 1, FF)], axis=2)
    b_w_s = b_w_s.astype(bf)
    kv = _mem_kv(mems, mem_g, w_mem_kv).reshape(DEPTH * batch, MEM_LEN, 2 * MEM_W)

    def mixer_weights(i):
        return (a_w_in, a_w_out) if i % 2 == 0 else (b_w_in, b_w_out)

    w_in, w_out = (w[0].astype(bf) for w in mixer_weights(0))
    w_up, w_down = ffn_w_up[0].astype(bf), ffn_w_down[0].astype(bf)

    for i in range(DEPTH):
        j = i // 2
        if i % 2 == 0:
            proj = _in_proj_a(xs, mix_g, i, w_in, tm=512, tn=1280)
            tok = _dilated_attention(proj, batch=batch)
            xs = _mix_out_a(xs, tok, proj, kv, w_out, i, tm=512)
        else:
            xs = _mixer_b(xs, mix_g, w_in, kv, v_norm_g, b_w_s, s_bias_t, w_out, i, j,
                          tm=512, tn=1792)
        casts = []
        if i + 1 < DEPTH:
            nxt_in, nxt_out = mixer_weights(i + 1)
            casts = [(nxt_in, (i + 1) // 2), (nxt_out, (i + 1) // 2), (ffn_w_up, i + 1),
                     (ffn_w_down, i + 1)]
        xs, nxt = _conv_ffn(xs, ffn_g, w_up, w_down, conv_p, final_g, i, casts,
                            tm=FFN_TM, tf=FFN_TF, final_norm=(i == DEPTH - 1))
        if nxt:
            w_in, w_out, w_up, w_down = nxt
    return xs.reshape(batch, seq, d)
```

```python
import functools

import numpy as np
import jax
import jax.numpy as jnp
from jax import lax
from jax.experimental import pallas as pl
from jax.experimental.pallas import tpu as pltpu

D_MODEL = 2048
SEQ = 2048
DEPTH = 4
EPS = 1e-6
NEG = -1e30

HEAD_DIM = 128
HEADS_PER_GROUP = 4
A_PATTERNS = ((128, 1), (512, 4), (2048, 16))
A_GROUPS = len(A_PATTERNS)
A_HEADS = HEADS_PER_GROUP * A_GROUPS
A_QKV_W = A_HEADS * HEAD_DIM
GROUP_W = HEADS_PER_GROUP * HEAD_DIM
QBLK = 128

CHUNK = 128
B_GROUPS = 12
B_W = B_GROUPS * 128

MEM_LEN = 256
MEM_HEADS = 4
MEM_W = MEM_HEADS * HEAD_DIM

A_IN = 3 * A_QKV_W + MEM_W
B_IN = 2 * B_W + MEM_W
FF = 5632

VMEM_LIMIT_BYTES = 62 * 1024 * 1024

_SQRT_HALF = 0.7071067811865476


def _params(n_axes):
    return pltpu.CompilerParams(
        dimension_semantics=("arbitrary",) * n_axes,
        vmem_limit_bytes=VMEM_LIMIT_BYTES,
    )


def _rms(x, g):
    y = x * lax.rsqrt(jnp.mean(x * x, axis=-1, keepdims=True) + EPS)
    return y * g


NORM_ROWS = 16


def _rms_to(dst_ref, x_ref, g, copy_ref=None):
    n = x_ref.shape[0]
    for r0 in range(0, n, NORM_ROWS):
        rows = slice(r0, r0 + NORM_ROWS)
        x = x_ref[rows, :]
        dst_ref[rows, :] = _rms(x, g).astype(dst_ref.dtype)
        if copy_ref is not None:
            copy_ref[rows, :] = x


def _gelu(x):
    return 0.5 * x * (1.0 + lax.erf(x * _SQRT_HALF))


def _alibi_slopes():
    return (2.0 ** (-8.0 * (np.arange(A_HEADS) + 1) / A_HEADS)).astype(np.float32)


def _layer_spec(block, index_map, layer):
    return pl.BlockSpec((None,) + block, lambda *g: (layer,) + index_map(*g))


def _resident(shape):
    return pl.BlockSpec(shape, lambda *g: (0,) * len(shape), pipeline_mode=pl.Buffered(1))


N_SLABS = 4


def _in_proj_a_kernel(x_ref, g_ref, w_ref, *refs, tn):
    group_refs = refs[:A_GROUPS]
    qm_ref, h_ref, slab_ref = refs[A_GROUPS:]
    tm = x_ref.shape[0]
    _rms_to(h_ref, x_ref, g_ref[...])
    n_slab = 0
    for c in reversed(range(w_ref.shape[1] // tn)):
        acc = jnp.dot(h_ref[...], w_ref[:, c * tn:(c + 1) * tn],
                      preferred_element_type=jnp.float32)
        for cc in range(tn // HEAD_DIM):
            col = c * (tn // HEAD_DIM) + cc
            slab = acc[:, cc * HEAD_DIM:(cc + 1) * HEAD_DIM]
            which, rest = divmod(col, A_HEADS)
            if which == 3:
                qm_ref[rest] = slab.astype(qm_ref.dtype)
                continue
            grp, head = divmod(rest, HEADS_PER_GROUP)
            dst = group_refs[grp]
            dilation = A_PATTERNS[grp][1]
            if dilation == 1:
                dst[which * HEADS_PER_GROUP + head, 0] = slab.astype(dst.dtype)
                continue
            buf = n_slab % N_SLABS
            n_slab += 1
            slab_ref[buf] = slab
            for r in range(dilation):
                rows = slab_ref[buf, pl.ds(r, tm // dilation, stride=dilation), :]
                dst[which * HEADS_PER_GROUP + head, r] = rows.astype(dst.dtype)


def _in_proj_a(x, g_all, layer_g, w, *, tm, tn):
    m, d = x.shape
    n = w.shape[1]
    batch = m // SEQ
    tiles_per_seq = SEQ // tm
    assert m % tm == 0 and SEQ % tm == 0 and n == A_IN and n % tn == 0 and tn % HEAD_DIM == 0
    out_specs, out_shapes = [], []
    for _, dilation in A_PATTERNS:
        assert tm % (dilation * BF16_ROWS) == 0
        out_specs.append(pl.BlockSpec(
            (3 * HEADS_PER_GROUP, None, dilation, tm // dilation, HEAD_DIM),
            lambda i: (0, i // tiles_per_seq, 0, i % tiles_per_seq, 0)))
        out_shapes.append(jax.ShapeDtypeStruct(
            (3 * HEADS_PER_GROUP, batch, dilation, SEQ // dilation, HEAD_DIM), jnp.bfloat16))
    out_specs.append(pl.BlockSpec((MEM_HEADS, tm, HEAD_DIM), lambda i: (0, i, 0)))
    out_shapes.append(jax.ShapeDtypeStruct((MEM_HEADS, m, HEAD_DIM), jnp.bfloat16))
    outs = pl.pallas_call(
        functools.partial(_in_proj_a_kernel, tn=tn),
        grid=(m // tm,),
        in_specs=[
            pl.BlockSpec((tm, d), lambda i: (i, 0)),
            _layer_spec((1, d), lambda i: (0, 0), layer_g),
            _resident(w.shape),
        ],
        out_specs=out_specs,
        out_shape=out_shapes,
        scratch_shapes=[pltpu.VMEM((tm, d), jnp.bfloat16),
                        pltpu.VMEM((N_SLABS, tm, HEAD_DIM), jnp.float32)],
        compiler_params=_params(1),
        name="in_proj_a",
    )(x, g_all, w)
    return outs[:A_GROUPS], outs[A_GROUPS]


def _mem_kv_kernel(x_ref, g_ref, w_ref, o_ref):
    h = _rms(x_ref[...], g_ref[...]).astype(jnp.bfloat16)
    w = w_ref[...].astype(jnp.bfloat16)
    o_ref[...] = jnp.dot(h, w, preferred_element_type=jnp.float32).astype(o_ref.dtype)


def _mem_kv(mems, g_all, w_all):
    m, d = mems.shape
    layers, _, n = w_all.shape
    return pl.pallas_call(
        _mem_kv_kernel,
        grid=(layers,),
        in_specs=[
            pl.BlockSpec((m, d), lambda l: (0, 0)),
            pl.BlockSpec((None, 1, d), lambda l: (l, 0, 0)),
            pl.BlockSpec((None, d, n), lambda l: (l, 0, 0)),
        ],
        out_specs=pl.BlockSpec((None, m, n), lambda l: (l, 0, 0)),
        out_shape=jax.ShapeDtypeStruct((layers, m, n), jnp.bfloat16),
        compiler_params=_params(1),
        name="mem_kv",
    )(mems, g_all, w_all)


def _group_geometry(grp):
    window, dilation = A_PATTERNS[grp]
    n_side = (window // 2) // dilation
    length = SEQ // dilation
    kw = min(length, QBLK + 2 * n_side)
    return dilation, n_side, length, kw


def _key_start(qi, n_side, length, kw):
    return min(max(qi * QBLK - n_side, 0), length - kw)


def _dilated_attn_kernel(slopes_ref, *refs):
    qkv_refs = refs[:3 * A_GROUPS]
    tok_ref = refs[3 * A_GROUPS]
    o_scr, lse_scr, bias_scr = refs[3 * A_GROUPS + 1:3 * A_GROUPS + 4]
    stage = refs[3 * A_GROUPS + 4:]
    head = pl.program_id(1)
    scale = HEAD_DIM ** -0.5

    for grp in range(A_GROUPS):
        dilation, n_side, length, kw = _group_geometry(grp)
        nblk = length // QBLK
        q_ref, k_ref, v_ref = qkv_refs[3 * grp:3 * grp + 3]
        vs_ref = stage[grp]
        slope = slopes_ref[grp, head]

        vs_ref[:, :, 0:HEAD_DIM] = v_ref[...]
        vs_ref[:, :, HEAD_DIM:2 * HEAD_DIM] = jnp.ones((dilation, length, HEAD_DIM), jnp.bfloat16)

        offsets = sorted({qi * QBLK - _key_start(qi, n_side, length, kw) for qi in range(nblk)})
        for t, off in enumerate(offsets):
            rel = (lax.broadcasted_iota(jnp.int32, (QBLK, kw), 1)
                   - lax.broadcasted_iota(jnp.int32, (QBLK, kw), 0) - off)
            dist = jnp.abs(rel)
            alibi = (-slope) * (dist * dilation).astype(jnp.float32)
            bias_scr[grp, t, :, 0:kw] = jnp.where(dist <= n_side, alibi, NEG)

        for r in range(dilation):
            for qi in range(nblk):
                k0 = _key_start(qi, n_side, length, kw)
                t = offsets.index(qi * QBLK - k0)
                q = q_ref[r, qi * QBLK:(qi + 1) * QBLK, :]
                k = k_ref[r, k0:k0 + kw, :]
                v1 = vs_ref[r, k0:k0 + kw, :]
                bias = bias_scr[grp, t, :, 0:kw]
                s = lax.dot_general(q, k, (((1,), (1,)), ((), ())),
                                    preferred_element_type=jnp.float32) * scale
                s = jnp.where(bias > 0.5 * NEG, s + bias, NEG)
                m = jnp.max(s, axis=-1, keepdims=True)
                p = jnp.exp(s - m).astype(jnp.bfloat16)
                ol = jnp.dot(p, v1, preferred_element_type=jnp.float32)
                l = ol[:, HEAD_DIM:]
                start = qi * QBLK * dilation + r
                dst = pl.ds(start, QBLK, stride=dilation) if dilation > 1 else pl.ds(start, QBLK)
                o_scr[grp, dst, :] = ol[:, :HEAD_DIM] / l
                lse_scr[grp, dst, :] = m + jnp.log(l)

    rows_per_step = 256
    for c in range(SEQ // rows_per_step):
        rows = slice(c * rows_per_step, (c + 1) * rows_per_step)
        lses = [lse_scr[grp, rows, :] for grp in range(A_GROUPS)]
        mx = functools.reduce(jnp.maximum, lses)
        es = [jnp.exp(l - mx) for l in lses]
        den = functools.reduce(lambda a, b: a + b, es)
        tok = sum((e / den) * o_scr[grp, rows, :] for grp, e in enumerate(es))
        tok_ref[rows, :] = tok.astype(tok_ref.dtype)


def _dilated_attention(qkv, *, batch):
    slopes = jnp.asarray(_alibi_slopes().reshape(A_GROUPS, HEADS_PER_GROUP))
    in_specs, operands, stage = [], [], []
    for grp in range(A_GROUPS):
        dilation, _, length, _ = _group_geometry(grp)
        for which in range(3):
            in_specs.append(pl.BlockSpec(
                (None, None, dilation, length, HEAD_DIM),
                lambda b, h, sl, which=which: (which * HEADS_PER_GROUP + h, b, 0, 0, 0)))
            operands.append(qkv[grp])
        stage.append(pltpu.VMEM((dilation, length, 2 * HEAD_DIM), jnp.bfloat16))
    max_kw = max(_group_geometry(grp)[3] for grp in range(A_GROUPS))
    return pl.pallas_call(
        _dilated_attn_kernel,
        grid_spec=pltpu.PrefetchScalarGridSpec(
            num_scalar_prefetch=1,
            grid=(batch, HEADS_PER_GROUP),
            in_specs=in_specs,
            out_specs=pl.BlockSpec((SEQ, HEAD_DIM), lambda b, h, sl: (b, h)),
            scratch_shapes=[
                pltpu.VMEM((A_GROUPS, SEQ, HEAD_DIM), jnp.float32),
                pltpu.VMEM((A_GROUPS, SEQ, HEAD_DIM), jnp.float32),
                pltpu.VMEM((A_GROUPS, 3, QBLK, max_kw), jnp.float32),
            ] + stage,
        ),
        out_shape=jax.ShapeDtypeStruct((batch * SEQ, GROUP_W), jnp.bfloat16),
        compiler_params=_params(2),
        name="dilated_attn",
    )(slopes, *operands)


def _mem_attention(q_heads, kv_ref):
    scale = HEAD_DIM ** -0.5
    outs = []
    for h, q in enumerate(q_heads):
        k = kv_ref[:, h * HEAD_DIM:(h + 1) * HEAD_DIM]
        v = kv_ref[:, MEM_W + h * HEAD_DIM:MEM_W + (h + 1) * HEAD_DIM]
        s = lax.dot_general(q, k, (((1,), (1,)), ((), ())),
                            preferred_element_type=jnp.float32) * scale
        m = jnp.max(s, axis=-1, keepdims=True)
        p = jnp.exp(s - m)
        l = jnp.sum(p, axis=-1, keepdims=True)
        o = jnp.dot(p.astype(jnp.bfloat16), v, preferred_element_type=jnp.float32)
        outs.append(o / l)
    return jnp.concatenate(outs, axis=-1)


def _mix_out_a_kernel(x_ref, tok_ref, qm_ref, kv_ref, w_ref, out_ref):
    q_heads = [qm_ref[h] for h in range(MEM_HEADS)]
    mem_out = _mem_attention(q_heads, kv_ref).astype(jnp.bfloat16)
    cat = jnp.concatenate([tok_ref[...], mem_out], axis=-1)
    out_ref[...] = x_ref[...] + jnp.dot(cat, w_ref[...], preferred_element_type=jnp.float32)


def _kv_spec(layer, batch, tiles_per_seq):
    return pl.BlockSpec((None, MEM_LEN, 2 * MEM_W),
                        lambda i: (layer * batch + i // tiles_per_seq, 0, 0))


def _mix_out_a(x, tok, qm, kv, w_out, layer, *, tm):
    m, d = x.shape
    tiles_per_seq = SEQ // tm
    batch = m // SEQ
    row = lambda i: (i, 0)
    return pl.pallas_call(
        _mix_out_a_kernel,
        grid=(m // tm,),
        in_specs=[
            pl.BlockSpec((tm, d), row),
            pl.BlockSpec((tm, GROUP_W), row),
            pl.BlockSpec((MEM_HEADS, tm, HEAD_DIM), lambda i: (0, i, 0)),
            _kv_spec(layer, batch, tiles_per_seq),
            pl.BlockSpec(w_out.shape, lambda i: (0, 0)),
        ],
        out_specs=pl.BlockSpec((tm, d), row),
        out_shape=jax.ShapeDtypeStruct((m, d), jnp.float32),
        compiler_params=_params(1),
        name="mix_out_a",
    )(x, tok, qm, kv, w_out)


def _mixer_b_kernel(x_ref, g_ref, win_ref, vg_ref, ws_ref, sb_ref, kv_ref, wout_ref, out_ref,
                    h_ref, uvq_ref, tok_ref, *, tn):
    tm = x_ref.shape[0]
    _rms_to(h_ref, x_ref, g_ref[...])
    for c in range(B_IN // tn):
        acc = jnp.dot(h_ref[...], win_ref[:, c * tn:(c + 1) * tn],
                      preferred_element_type=jnp.float32)
        n_gelu = min(max(2 * B_W - c * tn, 0), tn)
        if n_gelu:
            uvq_ref[:, c * tn:c * tn + n_gelu] = _gelu(acc[:, :n_gelu])
        if n_gelu < tn:
            uvq_ref[:, c * tn + n_gelu:(c + 1) * tn] = acc[:, n_gelu:]

    vn = _rms(uvq_ref[:, B_W:2 * B_W], vg_ref[...]).astype(jnp.bfloat16)
    for c in range(tm // CHUNK):
        rows = slice(c * CHUNK, (c + 1) * CHUNK)
        for g in range(B_GROUPS):
            cols = slice(g * 128, (g + 1) * 128)
            mixed = jnp.dot(ws_ref[g], vn[rows, cols], preferred_element_type=jnp.float32)
            mixed = mixed + sb_ref[:, g:g + 1]
            tok_ref[rows, cols] = (uvq_ref[rows, cols] * mixed).astype(jnp.bfloat16)
    qm = uvq_ref[:, 2 * B_W:].astype(jnp.bfloat16)
    q_heads = [qm[:, hd * HEAD_DIM:(hd + 1) * HEAD_DIM] for hd in range(MEM_HEADS)]
    mem_out = _mem_attention(q_heads, kv_ref).astype(jnp.bfloat16)
    cat = jnp.concatenate([tok_ref[...], mem_out], axis=-1)
    out_ref[...] = x_ref[...] + jnp.dot(cat, wout_ref[...], preferred_element_type=jnp.float32)


def _mixer_b(x, g_all, w_in, kv, v_norm_g_all, w_s_all, s_bias_t_all, w_out, layer, layer_b, *,
             tm, tn):
    m, d = x.shape
    assert m % tm == 0 and SEQ % tm == 0 and tm % CHUNK == 0 and B_IN % tn == 0
    tiles_per_seq = SEQ // tm
    batch = m // SEQ
    row = lambda i: (i, 0)
    const2 = lambda i: (0, 0)
    return pl.pallas_call(
        functools.partial(_mixer_b_kernel, tn=tn),
        grid=(m // tm,),
        in_specs=[
            pl.BlockSpec((tm, d), row),
            _layer_spec((1, d), const2, layer),
            _resident(w_in.shape),
            _layer_spec((1, B_W), const2, layer_b),
            _layer_spec(w_s_all.shape[1:], lambda i: (0, 0, 0), layer_b),
            _layer_spec(s_bias_t_all.shape[1:], const2, layer_b),
            _kv_spec(layer, batch, tiles_per_seq),
            _resident(w_out.shape),
        ],
        out_specs=pl.BlockSpec((tm, d), row),
        out_shape=jax.ShapeDtypeStruct((m, d), jnp.float32),
        scratch_shapes=[
            pltpu.VMEM((tm, d), jnp.bfloat16),
            pltpu.VMEM((tm, B_IN), jnp.float32),
            pltpu.VMEM((tm, B_W), jnp.bfloat16),
        ],
        compiler_params=_params(1),
        name="mixer_b",
    )(x, g_all, w_in, v_norm_g_all, w_s_all, s_bias_t_all, kv, w_out)


BF16_ROWS = 16
HALO = BF16_ROWS
FFN_TM = 1024
FFN_TF = 512
MXU_N = 256


N_FFN_IN = 9
CONV_TAPS = 3


def _conv_ffn_kernel(*refs, n_casts, tiles_per_seq, final_norm):
    (x_hbm, xp_ref, xn_ref, g_ref, wg_ref, wv_ref, cp_ref, wd_ref,
     fg_ref) = refs[:N_FFN_IN]
    cast_in = refs[N_FFN_IN:N_FFN_IN + n_casts]
    out_ref = refs[N_FFN_IN + n_casts]
    cast_out = refs[N_FFN_IN + n_casts + 1:N_FFN_IN + 2 * n_casts + 1]
    h_ref, ag_ref, av_ref, x_ref, x_sem = refs[N_FFN_IN + 2 * n_casts + 1:]
    i = pl.program_id(0)
    j = pl.program_id(1)
    n_tiles = pl.num_programs(0)
    nj = pl.num_programs(1)
    tm = x_ref.shape[0]

    def x_copy(tile):
        return pltpu.make_async_copy(x_hbm.at[pl.ds(tile * tm, tm), :], x_ref, x_sem)

    @pl.when(jnp.logical_and(i == 0, j == 0))
    def _():
        x_copy(0).start()

    @pl.when(j == 0)
    def _():
        x_copy(i).wait()
        g = g_ref[...]
        _rms_to(h_ref, x_ref, g, copy_ref=out_ref)
        first = (i % tiles_per_seq) == 0
        last = (i % tiles_per_seq) == tiles_per_seq - 1
        r = lax.broadcasted_iota(jnp.int32, (HALO, 1), 0)
        take_next = jnp.logical_and(r == 0, jnp.logical_not(last))
        take_prev = jnp.logical_and(r == HALO - 1, jnp.logical_not(first))
        halo = jnp.where(take_next, _rms(xn_ref[...], g),
                         jnp.where(take_prev, _rms(xp_ref[...], g), 0.0))
        h_ref[tm:tm + HALO, :] = halo.astype(jnp.bfloat16)

    @pl.when(jnp.logical_and(j == 1, i + 1 < n_tiles))
    def _():
        x_copy(i + 1).start()

    for src, dst in zip(cast_in, cast_out):
        dst[...] = src[...].astype(jnp.bfloat16)

    def up(w_ref, a_ref):
        a = jnp.dot(h_ref[...], w_ref[...], preferred_element_type=jnp.float32)
        a_ref[HALO:HALO + tm, :] = a[0:tm]
        a_ref[0:HALO, :] = a[tm:tm + HALO]
        a_ref[HALO + tm:2 * HALO + tm, :] = a[tm:tm + HALO]

    def conv(a_ref, half, cols):
        cp = cp_ref[half]
        return (a_ref[HALO - 1:HALO - 1 + tm, cols] * cp[0:1, cols]
                + a_ref[HALO:HALO + tm, cols] * cp[1:2, cols]
                + a_ref[HALO + 1:HALO + 1 + tm, cols] * cp[2:3, cols]
                + cp[CONV_TAPS:CONV_TAPS + 1, cols])

    up(wg_ref, ag_ref)
    up(wv_ref, av_ref)
    tf = wg_ref.shape[1]
    upd = None
    for c in range(tf // MXU_N):
        cols = slice(c * MXU_N, (c + 1) * MXU_N)
        act = (_gelu(conv(ag_ref, 0, cols)) * conv(av_ref, 1, cols)).astype(jnp.bfloat16)
        part = jnp.dot(act, wd_ref[cols, :], preferred_element_type=jnp.float32)
        upd = part if upd is None else upd + part
    out_ref[...] += upd

    if final_norm:
        @pl.when(j == nj - 1)
        def _():
            _rms_to(out_ref, out_ref, fg_ref[...])


def _conv_ffn(x, g_all, w_up, w_down, cp_all, final_g, layer, casts, *, tm, tf, final_norm):
    m, d = x.shape
    nf = FF // tf
    assert m % tm == 0 and FF % tf == 0 and SEQ % tm == 0 and tm % HALO == 0
    assert nf >= 2
    n_tiles = m // tm
    n_steps = n_tiles * nf
    tiles_per_seq = SEQ // tm
    hb = tm // HALO
    n_hblocks = m // HALO

    cast_in_specs, cast_out_specs, cast_out_shapes = [], [], []
    for src, src_layer in casts:
        _, rows, width = src.shape
        block_rows = BF16_ROWS * pl.cdiv(rows, BF16_ROWS * n_steps)
        n_blocks = rows // block_rows
        assert rows % block_rows == 0 and n_blocks <= n_steps

        def imap(i, j, n_blocks=n_blocks):
            return (jnp.minimum(i * nf + j, n_blocks - 1), 0)

        cast_in_specs.append(_layer_spec((block_rows, width), imap, src_layer))
        cast_out_specs.append(pl.BlockSpec((block_rows, width), imap))
        cast_out_shapes.append(jax.ShapeDtypeStruct((rows, width), jnp.bfloat16))

    outs = pl.pallas_call(
        functools.partial(_conv_ffn_kernel, n_casts=len(casts),
                          tiles_per_seq=tiles_per_seq, final_norm=final_norm),
        grid=(n_tiles, nf),
        in_specs=[
            pl.BlockSpec(memory_space=pl.ANY),
            pl.BlockSpec((HALO, d), lambda i, j: (jnp.maximum(i * hb - 1, 0), 0)),
            pl.BlockSpec((HALO, d), lambda i, j: (jnp.minimum((i + 1) * hb, n_hblocks - 1), 0)),
            _layer_spec((1, d), lambda i, j: (0, 0), layer),
            pl.BlockSpec((d, tf), lambda i, j: (0, j)),
            pl.BlockSpec((d, tf), lambda i, j: (0, nf + j)),
            _layer_spec((2, CONV_TAPS + 1, tf), lambda i, j: (0, 0, j), layer),
            pl.BlockSpec((tf, d), lambda i, j: (j, 0)),
            pl.BlockSpec((1, d), lambda i, j: (0, 0)),
        ] + cast_in_specs,
        out_specs=[pl.BlockSpec((tm, d), lambda i, j: (i, 0))] + cast_out_specs,
        out_shape=[jax.ShapeDtypeStruct((m, d), jnp.float32)] + cast_out_shapes,
        scratch_shapes=[
            pltpu.VMEM((tm + HALO, d), jnp.bfloat16),
            pltpu.VMEM((tm + 2 * HALO, tf), jnp.float32),
            pltpu.VMEM((tm + 2 * HALO, tf), jnp.float32),
            pltpu.VMEM((tm, d), jnp.float32),
            pltpu.SemaphoreType.DMA(()),
        ],
        compiler_params=_params(2),
        name="conv_ffn",
    )(x, x, x, g_all, w_up, w_up, cp_all, w_down, final_g, *[c[0] for c in casts])
    return outs[0], outs[1:]


def kernel(x, mem, mix_norm_g, ffn_norm_g, mem_norm_g, w_mem_kv, a_w_in, a_w_out, b_w_in,
           b_v_norm_g, b_w_s, b_s_bias, b_w_out, ffn_w_up, ffn_conv_w, ffn_conv_b, ffn_w_down,
           final_norm_g):
    batch, seq, d = x.shape
    assert (seq, d) == (SEQ, D_MODEL)
    bf = jnp.bfloat16
    xs = x.reshape(batch * seq, d)
    mems = mem.reshape(batch * MEM_LEN, d)

    mix_g = mix_norm_g.reshape(DEPTH, 1, d)
    ffn_g = ffn_norm_g.reshape(DEPTH, 1, d)
    mem_g = mem_norm_g.reshape(DEPTH, 1, d)
    final_g = final_norm_g.reshape(1, d)
    v_norm_g = b_v_norm_g.reshape(-1, 1, B_W)
    s_bias_t = jnp.swapaxes(b_s_bias, 1, 2)
    conv_p = jnp.concatenate([ffn_conv_w.reshape(DEPTH, CONV_TAPS, 2, FF).transpose(0, 2, 1, 3),
                              ffn_conv_b.reshape(DEPTH, 2, 1, FF)], axis=2)
    b_w_s = b_w_s.astype(bf)
    kv = _mem_kv(mems, mem_g, w_mem_kv).reshape(DEPTH * batch, MEM_LEN, 2 * MEM_W)

    def mixer_weights(i):
        return (a_w_in, a_w_out) if i % 2 == 0 else (b_w_in, b_w_out)

    w_in, w_out = (w[0].astype(bf) for w in mixer_weights(0))
    w_up, w_down = ffn_w_up[0].astype(bf), ffn_w_down[0].astype(bf)

    for i in range(DEPTH):
        j = i // 2
        if i % 2 == 0:
            qkv, qm = _in_proj_a(xs, mix_g, i, w_in, tm=512, tn=1280)
            tok = _dilated_attention(qkv, batch=batch)
            xs = _mix_out_a(xs, tok, qm, kv, w_out, i, tm=512)
        else:
            xs = _mixer_b(xs, mix_g, w_in, kv, v_norm_g, b_w_s, s_bias_t, w_out, i, j,
                          tm=512, tn=1792)
        casts = []
        if i + 1 < DEPTH:
            nxt_in, nxt_out = mixer_weights(i + 1)
            casts = [(nxt_in, (i + 1) // 2), (nxt_out, (i + 1) // 2), (ffn_w_up, i + 1),
                     (ffn_w_down, i + 1)]
        xs, nxt = _conv_ffn(xs, ffn_g, w_up, w_down, conv_p, final_g, i, casts,
                            tm=FFN_TM, tf=FFN_TF, final_norm=(i == DEPTH - 1))
        if nxt:
            w_in, w_out, w_up, w_down = nxt
    return xs.reshape(batch, seq, d)
```

```python
import functools

import numpy as np
import jax
import jax.numpy as jnp
from jax import lax
from jax.experimental import pallas as pl
from jax.experimental.pallas import tpu as pltpu

D_MODEL = 2048
SEQ = 2048
DEPTH = 4
EPS = 1e-6
NEG = -1e30

HEAD_DIM = 128
HEADS_PER_GROUP = 4
A_PATTERNS = ((128, 1), (512, 4), (2048, 16))
A_GROUPS = len(A_PATTERNS)
A_HEADS = HEADS_PER_GROUP * A_GROUPS
A_QKV_W = A_HEADS * HEAD_DIM
GROUP_W = HEADS_PER_GROUP * HEAD_DIM
QBLK = 128

CHUNK = 128
B_GROUPS = 12
B_W = B_GROUPS * 128

MEM_LEN = 256
MEM_HEADS = 4
MEM_W = MEM_HEADS * HEAD_DIM

A_IN = 3 * A_QKV_W + MEM_W
B_IN = 2 * B_W + MEM_W
FF = 5632

VMEM_LIMIT_BYTES = 62 * 1024 * 1024

_SQRT_HALF = 0.7071067811865476


def _params(n_axes):
    return pltpu.CompilerParams(
        dimension_semantics=("arbitrary",) * n_axes,
        vmem_limit_bytes=VMEM_LIMIT_BYTES,
    )


def _rms(x, g):
    y = x * lax.rsqrt(jnp.mean(x * x, axis=-1, keepdims=True) + EPS)
    return y * g


NORM_ROWS = 16


def _rms_to(dst_ref, x_ref, g, copy_ref=None):
    n = x_ref.shape[0]
    for r0 in range(0, n, NORM_ROWS):
        rows = slice(r0, r0 + NORM_ROWS)
        x = x_ref[rows, :]
        dst_ref[rows, :] = _rms(x, g).astype(dst_ref.dtype)
        if copy_ref is not None:
            copy_ref[rows, :] = x


def _gelu(x):
    return 0.5 * x * (1.0 + lax.erf(x * _SQRT_HALF))


def _alibi_slopes():
    return (2.0 ** (-8.0 * (np.arange(A_HEADS) + 1) / A_HEADS)).astype(np.float32)


def _layer_spec(block, index_map, layer):
    return pl.BlockSpec((None,) + block, lambda *g: (layer,) + index_map(*g))


def _resident(shape):
    return pl.BlockSpec(shape, lambda *g: (0,) * len(shape), pipeline_mode=pl.Buffered(1))


BF16_ROWS = 16


def _cast_specs(casts, n_steps, step_of):
    in_specs, out_specs, out_shapes = [], [], []
    for src, src_layer in casts:
        _, rows, width = src.shape
        block_rows = BF16_ROWS * pl.cdiv(rows, BF16_ROWS * n_steps)
        n_blocks = rows // block_rows
        assert rows % block_rows == 0 and n_blocks <= n_steps

        def imap(*g, n_blocks=n_blocks):
            return (jnp.minimum(step_of(*g), n_blocks - 1), 0)

        in_specs.append(_layer_spec((block_rows, width), imap, src_layer))
        out_specs.append(pl.BlockSpec((block_rows, width), imap))
        out_shapes.append(jax.ShapeDtypeStruct((rows, width), jnp.bfloat16))
    return in_specs, out_specs, out_shapes


def _run_casts(cast_in, cast_out):
    for src, dst in zip(cast_in, cast_out):
        dst[...] = src[...].astype(jnp.bfloat16)


N_SLABS = 4
GATHER_STRIDE = 4


def _in_proj_a_kernel(x_ref, g_ref, w_ref, *refs, tn, n_casts):
    cast_in, refs = refs[:n_casts], refs[n_casts:]
    group_refs = refs[:A_GROUPS]
    qm_ref = refs[A_GROUPS]
    cast_out = refs[A_GROUPS + 1:A_GROUPS + 1 + n_casts]
    h_ref, slab_ref, part_ref = refs[A_GROUPS + 1 + n_casts:]
    tm = x_ref.shape[0]
    _run_casts(cast_in, cast_out)
    _rms_to(h_ref, x_ref, g_ref[...])
    n_slab = 0
    for c in reversed(range(w_ref.shape[1] // tn)):
        acc = jnp.dot(h_ref[...], w_ref[:, c * tn:(c + 1) * tn],
                      preferred_element_type=jnp.float32)
        for cc in range(tn // HEAD_DIM):
            col = c * (tn // HEAD_DIM) + cc
            slab = acc[:, cc * HEAD_DIM:(cc + 1) * HEAD_DIM]
            which, rest = divmod(col, A_HEADS)
            if which == 3:
                qm_ref[rest] = slab.astype(qm_ref.dtype)
                continue
            grp, head = divmod(rest, HEADS_PER_GROUP)
            dst = group_refs[grp]
            dilation = A_PATTERNS[grp][1]
            if dilation == 1:
                dst[which * HEADS_PER_GROUP + head, 0] = slab.astype(dst.dtype)
                continue
            buf = n_slab % N_SLABS
            n_slab += 1
            slab_ref[buf] = slab
            out_col = which * HEADS_PER_GROUP + head
            if dilation <= GATHER_STRIDE:
                for r in range(dilation):
                    rows = slab_ref[buf, pl.ds(r, tm // dilation, stride=dilation), :]
                    dst[out_col, r] = rows.astype(dst.dtype)
                continue
            outer = dilation // GATHER_STRIDE
            for r1 in range(GATHER_STRIDE):
                part_ref[buf, r1] = slab_ref[buf, pl.ds(r1, tm // GATHER_STRIDE,
                                                        stride=GATHER_STRIDE), :]
            for r1 in range(GATHER_STRIDE):
                for k in range(outer):
                    rows = part_ref[buf, r1, pl.ds(k, tm // dilation, stride=outer), :]
                    dst[out_col, r1 + GATHER_STRIDE * k] = rows.astype(dst.dtype)


def _in_proj_a(x, g_all, layer_g, w, casts, *, tm, tn):
    m, d = x.shape
    n = w.shape[1]
    batch = m // SEQ
    tiles_per_seq = SEQ // tm
    assert m % tm == 0 and SEQ % tm == 0 and n == A_IN and n % tn == 0 and tn % HEAD_DIM == 0
    out_specs, out_shapes = [], []
    for _, dilation in A_PATTERNS:
        assert tm % (dilation * BF16_ROWS) == 0
        out_specs.append(pl.BlockSpec(
            (3 * HEADS_PER_GROUP, None, dilation, tm // dilation, HEAD_DIM),
            lambda i: (0, i // tiles_per_seq, 0, i % tiles_per_seq, 0)))
        out_shapes.append(jax.ShapeDtypeStruct(
            (3 * HEADS_PER_GROUP, batch, dilation, SEQ // dilation, HEAD_DIM), jnp.bfloat16))
    out_specs.append(pl.BlockSpec((MEM_HEADS, tm, HEAD_DIM), lambda i: (0, i, 0)))
    out_shapes.append(jax.ShapeDtypeStruct((MEM_HEADS, m, HEAD_DIM), jnp.bfloat16))
    cast_in_specs, cast_out_specs, cast_out_shapes = _cast_specs(casts, m // tm, lambda i: i)
    outs = pl.pallas_call(
        functools.partial(_in_proj_a_kernel, tn=tn, n_casts=len(casts)),
        grid=(m // tm,),
        in_specs=[
            pl.BlockSpec((tm, d), lambda i: (i, 0)),
            _layer_spec((1, d), lambda i: (0, 0), layer_g),
            _resident(w.shape),
        ] + cast_in_specs,
        out_specs=out_specs + cast_out_specs,
        out_shape=out_shapes + cast_out_shapes,
        scratch_shapes=[pltpu.VMEM((tm, d), jnp.bfloat16),
                        pltpu.VMEM((N_SLABS, tm, HEAD_DIM), jnp.float32),
                        pltpu.VMEM((N_SLABS, GATHER_STRIDE, tm // GATHER_STRIDE, HEAD_DIM),
                                   jnp.float32)],
        compiler_params=_params(1),
        name="in_proj_a",
    )(x, g_all, w, *[c[0] for c in casts])
    return outs[:A_GROUPS], outs[A_GROUPS], outs[A_GROUPS + 1:]


def _mem_kv_kernel(x_ref, g_ref, w_ref, o_ref):
    h = _rms(x_ref[...], g_ref[...]).astype(jnp.bfloat16)
    w = w_ref[...].astype(jnp.bfloat16)
    o_ref[...] = jnp.dot(h, w, preferred_element_type=jnp.float32).astype(o_ref.dtype)


def _mem_kv(mems, g_all, w_all):
    m, d = mems.shape
    layers, _, n = w_all.shape
    return pl.pallas_call(
        _mem_kv_kernel,
        grid=(layers,),
        in_specs=[
            pl.BlockSpec((m, d), lambda l: (0, 0)),
            pl.BlockSpec((None, 1, d), lambda l: (l, 0, 0)),
            pl.BlockSpec((None, d, n), lambda l: (l, 0, 0)),
        ],
        out_specs=pl.BlockSpec((None, m, n), lambda l: (l, 0, 0)),
        out_shape=jax.ShapeDtypeStruct((layers, m, n), jnp.bfloat16),
        compiler_params=_params(1),
        name="mem_kv",
    )(mems, g_all, w_all)


def _group_geometry(grp):
    window, dilation = A_PATTERNS[grp]
    n_side = (window // 2) // dilation
    length = SEQ // dilation
    kw = min(length, QBLK + 2 * n_side)
    return dilation, n_side, length, kw


def _key_start(qi, n_side, length, kw):
    return min(max(qi * QBLK - n_side, 0), length - kw)


def _dilated_attn_kernel(slopes_ref, *refs, n_casts):
    qkv_refs, refs = refs[:3 * A_GROUPS], refs[3 * A_GROUPS:]
    cast_in, refs = refs[:n_casts], refs[n_casts:]
    tok_ref, refs = refs[0], refs[1:]
    cast_out, refs = refs[:n_casts], refs[n_casts:]
    o_scr, lse_scr, bias_scr = refs[:3]
    stage = refs[3:]
    head = pl.program_id(1)
    scale = HEAD_DIM ** -0.5
    _run_casts(cast_in, cast_out)

    for grp in range(A_GROUPS):
        dilation, n_side, length, kw = _group_geometry(grp)
        nblk = length // QBLK
        q_ref, k_ref, v_ref = qkv_refs[3 * grp:3 * grp + 3]
        vs_ref = stage[grp]
        slope = slopes_ref[grp, head]

        vs_ref[:, :, 0:HEAD_DIM] = v_ref[...]
        vs_ref[:, :, HEAD_DIM:2 * HEAD_DIM] = jnp.ones((dilation, length, HEAD_DIM), jnp.bfloat16)

        offsets = sorted({qi * QBLK - _key_start(qi, n_side, length, kw) for qi in range(nblk)})
        for t, off in enumerate(offsets):
            rel = (lax.broadcasted_iota(jnp.int32, (QBLK, kw), 1)
                   - lax.broadcasted_iota(jnp.int32, (QBLK, kw), 0) - off)
            dist = jnp.abs(rel)
            alibi = (-slope) * (dist * dilation).astype(jnp.float32)
            bias_scr[grp, t, :, 0:kw] = jnp.where(dist <= n_side, alibi, NEG)

        for r in range(dilation):
            for qi in range(nblk):
                k0 = _key_start(qi, n_side, length, kw)
                t = offsets.index(qi * QBLK - k0)
                q = q_ref[r, qi * QBLK:(qi + 1) * QBLK, :]
                k = k_ref[r, k0:k0 + kw, :]
                v1 = vs_ref[r, k0:k0 + kw, :]
                bias = bias_scr[grp, t, :, 0:kw]
                s = lax.dot_general(q, k, (((1,), (1,)), ((), ())),
                                    preferred_element_type=jnp.float32) * scale
                s = jnp.where(bias > 0.5 * NEG, s + bias, NEG)
                m = jnp.max(s, axis=-1, keepdims=True)
                p = jnp.exp(s - m).astype(jnp.bfloat16)
                ol = jnp.dot(p, v1, preferred_element_type=jnp.float32)
                l = ol[:, HEAD_DIM:]
                start = qi * QBLK * dilation + r
                dst = pl.ds(start, QBLK, stride=dilation) if dilation > 1 else pl.ds(start, QBLK)
                o_scr[grp, dst, :] = ol[:, :HEAD_DIM] / l
                lse_scr[grp, dst, :] = m + jnp.log(l)

    rows_per_step = 256
    for c in range(SEQ // rows_per_step):
        rows = slice(c * rows_per_step, (c + 1) * rows_per_step)
        lses = [lse_scr[grp, rows, :] for grp in range(A_GROUPS)]
        mx = functools.reduce(jnp.maximum, lses)
        es = [jnp.exp(l - mx) for l in lses]
        den = functools.reduce(lambda a, b: a + b, es)
        tok = sum((e / den) * o_scr[grp, rows, :] for grp, e in enumerate(es))
        tok_ref[rows, :] = tok.astype(tok_ref.dtype)


def _dilated_attention(qkv, casts, *, batch):
    slopes = jnp.asarray(_alibi_slopes().reshape(A_GROUPS, HEADS_PER_GROUP))
    in_specs, operands, stage = [], [], []
    for grp in range(A_GROUPS):
        dilation, _, length, _ = _group_geometry(grp)
        for which in range(3):
            in_specs.append(pl.BlockSpec(
                (None, None, dilation, length, HEAD_DIM),
                lambda b, h, sl, which=which: (which * HEADS_PER_GROUP + h, b, 0, 0, 0)))
            operands.append(qkv[grp])
        stage.append(pltpu.VMEM((dilation, length, 2 * HEAD_DIM), jnp.bfloat16))
    max_kw = max(_group_geometry(grp)[3] for grp in range(A_GROUPS))
    cast_in_specs, cast_out_specs, cast_out_shapes = _cast_specs(
        casts, batch * HEADS_PER_GROUP, lambda b, h, sl: b * HEADS_PER_GROUP + h)
    outs = pl.pallas_call(
        functools.partial(_dilated_attn_kernel, n_casts=len(casts)),
        grid_spec=pltpu.PrefetchScalarGridSpec(
            num_scalar_prefetch=1,
            grid=(batch, HEADS_PER_GROUP),
            in_specs=in_specs + cast_in_specs,
            out_specs=[pl.BlockSpec((SEQ, HEAD_DIM), lambda b, h, sl: (b, h))] + cast_out_specs,
            scratch_shapes=[
                pltpu.VMEM((A_GROUPS, SEQ, HEAD_DIM), jnp.float32),
                pltpu.VMEM((A_GROUPS, SEQ, HEAD_DIM), jnp.float32),
                pltpu.VMEM((A_GROUPS, 3, QBLK, max_kw), jnp.float32),
            ] + stage,
        ),
        out_shape=[jax.ShapeDtypeStruct((batch * SEQ, GROUP_W), jnp.bfloat16)] + cast_out_shapes,
        compiler_params=_params(2),
        name="dilated_attn",
    )(slopes, *operands, *[c[0] for c in casts])
    return outs[0], outs[1:]


def _mem_attention(q_heads, kv_ref):
    scale = HEAD_DIM ** -0.5
    outs = []
    for h, q in enumerate(q_heads):
        k = kv_ref[:, h * HEAD_DIM:(h + 1) * HEAD_DIM]
        v = kv_ref[:, MEM_W + h * HEAD_DIM:MEM_W + (h + 1) * HEAD_DIM]
        s = lax.dot_general(q, k, (((1,), (1,)), ((), ())),
                            preferred_element_type=jnp.float32) * scale
        m = jnp.max(s, axis=-1, keepdims=True)
        p = jnp.exp(s - m)
        l = jnp.sum(p, axis=-1, keepdims=True)
        o = jnp.dot(p.astype(jnp.bfloat16), v, preferred_element_type=jnp.float32)
        outs.append(o / l)
    return jnp.concatenate(outs, axis=-1)


def _mix_out_a_kernel(x_ref, tok_ref, qm_ref, kv_ref, w_ref, out_ref):
    q_heads = [qm_ref[h] for h in range(MEM_HEADS)]
    mem_out = _mem_attention(q_heads, kv_ref).astype(jnp.bfloat16)
    cat = jnp.concatenate([tok_ref[...], mem_out], axis=-1)
    out_ref[...] = x_ref[...] + jnp.dot(cat, w_ref[...], preferred_element_type=jnp.float32)


def _kv_spec(layer, batch, tiles_per_seq):
    return pl.BlockSpec((None, MEM_LEN, 2 * MEM_W),
                        lambda i: (layer * batch + i // tiles_per_seq, 0, 0))


def _mix_out_a(x, tok, qm, kv, w_out, layer, *, tm):
    m, d = x.shape
    tiles_per_seq = SEQ // tm
    batch = m // SEQ
    row = lambda i: (i, 0)
    return pl.pallas_call(
        _mix_out_a_kernel,
        grid=(m // tm,),
        in_specs=[
            pl.BlockSpec((tm, d), row),
            pl.BlockSpec((tm, GROUP_W), row),
            pl.BlockSpec((MEM_HEADS, tm, HEAD_DIM), lambda i: (0, i, 0)),
            _kv_spec(layer, batch, tiles_per_seq),
            pl.BlockSpec(w_out.shape, lambda i: (0, 0)),
        ],
        out_specs=pl.BlockSpec((tm, d), row),
        out_shape=jax.ShapeDtypeStruct((m, d), jnp.float32),
        compiler_params=_params(1),
        name="mix_out_a",
    )(x, tok, qm, kv, w_out)


def _mixer_b_kernel(x_ref, g_ref, win_ref, vg_ref, ws_ref, sb_ref, kv_ref, wout_ref, out_ref,
                    h_ref, uvq_ref, tok_ref, *, tn):
    tm = x_ref.shape[0]
    _rms_to(h_ref, x_ref, g_ref[...])
    for c in range(B_IN // tn):
        acc = jnp.dot(h_ref[...], win_ref[:, c * tn:(c + 1) * tn],
                      preferred_element_type=jnp.float32)
        n_gelu = min(max(2 * B_W - c * tn, 0), tn)
        if n_gelu:
            uvq_ref[:, c * tn:c * tn + n_gelu] = _gelu(acc[:, :n_gelu])
        if n_gelu < tn:
            uvq_ref[:, c * tn + n_gelu:(c + 1) * tn] = acc[:, n_gelu:]

    vn = _rms(uvq_ref[:, B_W:2 * B_W], vg_ref[...]).astype(jnp.bfloat16)
    for c in range(tm // CHUNK):
        rows = slice(c * CHUNK, (c + 1) * CHUNK)
        for g in range(B_GROUPS):
            cols = slice(g * 128, (g + 1) * 128)
            mixed = jnp.dot(ws_ref[g], vn[rows, cols], preferred_element_type=jnp.float32)
            mixed = mixed + sb_ref[:, g:g + 1]
            tok_ref[rows, cols] = (uvq_ref[rows, cols] * mixed).astype(jnp.bfloat16)
    qm = uvq_ref[:, 2 * B_W:].astype(jnp.bfloat16)
    q_heads = [qm[:, hd * HEAD_DIM:(hd + 1) * HEAD_DIM] for hd in range(MEM_HEADS)]
    mem_out = _mem_attention(q_heads, kv_ref).astype(jnp.bfloat16)
    cat = jnp.concatenate([tok_ref[...], mem_out], axis=-1)
    out_ref[...] = x_ref[...] + jnp.dot(cat, wout_ref[...], preferred_element_type=jnp.float32)


def _mixer_b(x, g_all, w_in, kv, v_norm_g_all, w_s_all, s_bias_t_all, w_out, layer, layer_b, *,
             tm, tn):
    m, d = x.shape
    assert m % tm == 0 and SEQ % tm == 0 and tm % CHUNK == 0 and B_IN % tn == 0
    tiles_per_seq = SEQ // tm
    batch = m // SEQ
    row = lambda i: (i, 0)
    const2 = lambda i: (0, 0)
    return pl.pallas_call(
        functools.partial(_mixer_b_kernel, tn=tn),
        grid=(m // tm,),
        in_specs=[
            pl.BlockSpec((tm, d), row),
            _layer_spec((1, d), const2, layer),
            _resident(w_in.shape),
            _layer_spec((1, B_W), const2, layer_b),
            _layer_spec(w_s_all.shape[1:], lambda i: (0, 0, 0), layer_b),
            _layer_spec(s_bias_t_all.shape[1:], const2, layer_b),
            _kv_spec(layer, batch, tiles_per_seq),
            _resident(w_out.shape),
        ],
        out_specs=pl.BlockSpec((tm, d), row),
        out_shape=jax.ShapeDtypeStruct((m, d), jnp.float32),
        scratch_shapes=[
            pltpu.VMEM((tm, d), jnp.bfloat16),
            pltpu.VMEM((tm, B_IN), jnp.float32),
            pltpu.VMEM((tm, B_W), jnp.bfloat16),
        ],
        compiler_params=_params(1),
        name="mixer_b",
    )(x, g_all, w_in, v_norm_g_all, w_s_all, s_bias_t_all, kv, w_out)


HALO = BF16_ROWS
FFN_TM = 1024
FFN_TF = 512
MXU_N = 256


N_FFN_IN = 9
CONV_TAPS = 3


def _conv_ffn_kernel(*refs, n_casts, tiles_per_seq, final_norm):
    (x_hbm, xp_ref, xn_ref, g_ref, wg_ref, wv_ref, cp_ref, wd_ref,
     fg_ref) = refs[:N_FFN_IN]
    cast_in = refs[N_FFN_IN:N_FFN_IN + n_casts]
    out_ref = refs[N_FFN_IN + n_casts]
    cast_out = refs[N_FFN_IN + n_casts + 1:N_FFN_IN + 2 * n_casts + 1]
    h_ref, ag_ref, av_ref, x_ref, x_sem = refs[N_FFN_IN + 2 * n_casts + 1:]
    i = pl.program_id(0)
    j = pl.program_id(1)
    n_tiles = pl.num_programs(0)
    nj = pl.num_programs(1)
    tm = x_ref.shape[0]

    def x_copy(tile):
        return pltpu.make_async_copy(x_hbm.at[pl.ds(tile * tm, tm), :], x_ref, x_sem)

    @pl.when(jnp.logical_and(i == 0, j == 0))
    def _():
        x_copy(0).start()

    @pl.when(j == 0)
    def _():
        x_copy(i).wait()
        g = g_ref[...]
        _rms_to(h_ref, x_ref, g, copy_ref=out_ref)
        first = (i % tiles_per_seq) == 0
        last = (i % tiles_per_seq) == tiles_per_seq - 1
        r = lax.broadcasted_iota(jnp.int32, (HALO, 1), 0)
        take_next = jnp.logical_and(r == 0, jnp.logical_not(last))
        take_prev = jnp.logical_and(r == HALO - 1, jnp.logical_not(first))
        halo = jnp.where(take_next, _rms(xn_ref[...], g),
                         jnp.where(take_prev, _rms(xp_ref[...], g), 0.0))
        h_ref[tm:tm + HALO, :] = halo.astype(jnp.bfloat16)

    @pl.when(jnp.logical_and(j == 1, i + 1 < n_tiles))
    def _():
        x_copy(i + 1).start()

    _run_casts(cast_in, cast_out)

    def up(w_ref, a_ref):
        a = jnp.dot(h_ref[...], w_ref[...], preferred_element_type=jnp.float32)
        a_ref[HALO:HALO + tm, :] = a[0:tm]
        a_ref[0:HALO, :] = a[tm:tm + HALO]
        a_ref[HALO + tm:2 * HALO + tm, :] = a[tm:tm + HALO]

    def conv(a_ref, half, cols):
        cp = cp_ref[half]
        return (a_ref[HALO - 1:HALO - 1 + tm, cols] * cp[0:1, cols]
                + a_ref[HALO:HALO + tm, cols] * cp[1:2, cols]
                + a_ref[HALO + 1:HALO + 1 + tm, cols] * cp[2:3, cols]
                + cp[CONV_TAPS:CONV_TAPS + 1, cols])

    up(wg_ref, ag_ref)
    up(wv_ref, av_ref)
    tf = wg_ref.shape[1]
    upd = None
    for c in range(tf // MXU_N):
        cols = slice(c * MXU_N, (c + 1) * MXU_N)
        act = (_gelu(conv(ag_ref, 0, cols)) * conv(av_ref, 1, cols)).astype(jnp.bfloat16)
        part = jnp.dot(act, wd_ref[cols, :], preferred_element_type=jnp.float32)
        upd = part if upd is None else upd + part
    out_ref[...] += upd

    if final_norm:
        @pl.when(j == nj - 1)
        def _():
            _rms_to(out_ref, out_ref, fg_ref[...])


def _conv_ffn(x, g_all, w_up, w_down, cp_all, final_g, layer, casts, *, tm, tf, final_norm):
    m, d = x.shape
    nf = FF // tf
    assert m % tm == 0 and FF % tf == 0 and SEQ % tm == 0 and tm % HALO == 0
    assert nf >= 2
    n_tiles = m // tm
    n_steps = n_tiles * nf
    tiles_per_seq = SEQ // tm
    hb = tm // HALO
    n_hblocks = m // HALO

    cast_in_specs, cast_out_specs, cast_out_shapes = _cast_specs(
        casts, n_steps, lambda i, j: i * nf + j)

    outs = pl.pallas_call(
        functools.partial(_conv_ffn_kernel, n_casts=len(casts),
                          tiles_per_seq=tiles_per_seq, final_norm=final_norm),
        grid=(n_tiles, nf),
        in_specs=[
            pl.BlockSpec(memory_space=pl.ANY),
            pl.BlockSpec((HALO, d), lambda i, j: (jnp.maximum(i * hb - 1, 0), 0)),
            pl.BlockSpec((HALO, d), lambda i, j: (jnp.minimum((i + 1) * hb, n_hblocks - 1), 0)),
            _layer_spec((1, d), lambda i, j: (0, 0), layer),
            pl.BlockSpec((d, tf), lambda i, j: (0, j)),
            pl.BlockSpec((d, tf), lambda i, j: (0, nf + j)),
            _layer_spec((2, CONV_TAPS + 1, tf), lambda i, j: (0, 0, j), layer),
            pl.BlockSpec((tf, d), lambda i, j: (j, 0)),
            pl.BlockSpec((1, d), lambda i, j: (0, 0)),
        ] + cast_in_specs,
        out_specs=[pl.BlockSpec((tm, d), lambda i, j: (i, 0))] + cast_out_specs,
        out_shape=[jax.ShapeDtypeStruct((m, d), jnp.float32)] + cast_out_shapes,
        scratch_shapes=[
            pltpu.VMEM((tm + HALO, d), jnp.bfloat16),
            pltpu.VMEM((tm + 2 * HALO, tf), jnp.float32),
            pltpu.VMEM((tm + 2 * HALO, tf), jnp.float32),
            pltpu.VMEM((tm, d), jnp.float32),
            pltpu.SemaphoreType.DMA(()),
        ],
        compiler_params=_params(2),
        name="conv_ffn",
    )(x, x, x, g_all, w_up, w_up, cp_all, w_down, final_g, *[c[0] for c in casts])
    return outs[0], outs[1:]


def kernel(x, mem, mix_norm_g, ffn_norm_g, mem_norm_g, w_mem_kv, a_w_in, a_w_out, b_w_in,
           b_v_norm_g, b_w_s, b_s_bias, b_w_out, ffn_w_up, ffn_conv_w, ffn_conv_b, ffn_w_down,
           final_norm_g):
    batch, seq, d = x.shape
    assert (seq, d) == (SEQ, D_MODEL)
    bf = jnp.bfloat16
    xs = x.reshape(batch * seq, d)
    mems = mem.reshape(batch * MEM_LEN, d)

    mix_g = mix_norm_g.reshape(DEPTH, 1, d)
    ffn_g = ffn_norm_g.reshape(DEPTH, 1, d)
    mem_g = mem_norm_g.reshape(DEPTH, 1, d)
    final_g = final_norm_g.reshape(1, d)
    v_norm_g = b_v_norm_g.reshape(-1, 1, B_W)
    s_bias_t = jnp.swapaxes(b_s_bias, 1, 2)
    conv_p = jnp.concatenate([ffn_conv_w.reshape(DEPTH, CONV_TAPS, 2, FF).transpose(0, 2, 1, 3),
                              ffn_conv_b.reshape(DEPTH, 2, 1, FF)], axis=2)
    b_w_s = b_w_s.astype(bf)
    kv = _mem_kv(mems, mem_g, w_mem_kv).reshape(DEPTH * batch, MEM_LEN, 2 * MEM_W)

    def mixer_weights(i):
        return (a_w_in, a_w_out) if i % 2 == 0 else (b_w_in, b_w_out)

    w_in = a_w_in[0].astype(bf)
    w_out = w_up = w_down = None

    for i in range(DEPTH):
        j = i // 2
        if i % 2 == 0:
            first = i == 0
            qkv, qm, cast = _in_proj_a(xs, mix_g, i, w_in,
                                       [(a_w_out, 0), (ffn_w_down, 0)] if first else [],
                                       tm=512, tn=1280)
            if first:
                w_out, w_down = cast
            tok, cast = _dilated_attention(qkv, [(ffn_w_up, 0)] if first else [], batch=batch)
            if first:
                w_up, = cast
            xs = _mix_out_a(xs, tok, qm, kv, w_out, i, tm=1024)
        else:
            xs = _mixer_b(xs, mix_g, w_in, kv, v_norm_g, b_w_s, s_bias_t, w_out, i, j,
                          tm=512, tn=1792)
        casts = []
        if i + 1 < DEPTH:
            nxt_in, nxt_out = mixer_weights(i + 1)
            casts = [(nxt_in, (i + 1) // 2), (nxt_out, (i + 1) // 2), (ffn_w_up, i + 1),
                     (ffn_w_down, i + 1)]
        xs, nxt = _conv_ffn(xs, ffn_g, w_up, w_down, conv_p, final_g, i, casts,
                            tm=FFN_TM, tf=FFN_TF, final_norm=(i == DEPTH - 1))
        if nxt:
            w_in, w_out, w_up, w_down = nxt
    return xs.reshape(batch, seq, d)
```

```python
import functools

import numpy as np
import jax
import jax.numpy as jnp
from jax import lax
from jax.experimental import pallas as pl
from jax.experimental.pallas import tpu as pltpu

D_MODEL = 2048
SEQ = 2048
DEPTH = 4
EPS = 1e-6
NEG = -1e30

HEAD_DIM = 128
HEADS_PER_GROUP = 4
A_PATTERNS = ((128, 1), (512, 4), (2048, 16))
A_GROUPS = len(A_PATTERNS)
A_HEADS = HEADS_PER_GROUP * A_GROUPS
A_QKV_W = A_HEADS * HEAD_DIM
GROUP_W = HEADS_PER_GROUP * HEAD_DIM
QBLK = 128

CHUNK = 128
B_GROUPS = 12
B_W = B_GROUPS * 128

MEM_LEN = 256
MEM_HEADS = 4
MEM_W = MEM_HEADS * HEAD_DIM

A_IN = 3 * A_QKV_W + MEM_W
B_IN = 2 * B_W + MEM_W
FF = 5632

VMEM_LIMIT_BYTES = 62 * 1024 * 1024

_SQRT_HALF = 0.7071067811865476


def _params(n_axes):
    return pltpu.CompilerParams(
        dimension_semantics=("arbitrary",) * n_axes,
        vmem_limit_bytes=VMEM_LIMIT_BYTES,
    )


def _rms(x, g):
    y = x * lax.rsqrt(jnp.mean(x * x, axis=-1, keepdims=True) + EPS)
    return y * g


NORM_ROWS = 16


def _rms_to(dst_ref, x_ref, g, copy_ref=None):
    n = x_ref.shape[0]
    for r0 in range(0, n, NORM_ROWS):
        rows = slice(r0, r0 + NORM_ROWS)
        x = x_ref[rows, :]
        dst_ref[rows, :] = _rms(x, g).astype(dst_ref.dtype)
        if copy_ref is not None:
            copy_ref[rows, :] = x


def _gelu(x):
    return 0.5 * x * (1.0 + lax.erf(x * _SQRT_HALF))


def _alibi_slopes():
    return (2.0 ** (-8.0 * (np.arange(A_HEADS) + 1) / A_HEADS)).astype(np.float32)


def _layer_spec(block, index_map, layer):
    return pl.BlockSpec((None,) + block, lambda *g: (layer,) + index_map(*g))


def _resident(shape):
    return pl.BlockSpec(shape, lambda *g: (0,) * len(shape), pipeline_mode=pl.Buffered(1))


BF16_ROWS = 16


def _cast_specs(casts, n_steps, step_of):
    in_specs, out_specs, out_shapes = [], [], []
    for src, src_layer in casts:
        _, rows, width = src.shape
        block_rows = BF16_ROWS * pl.cdiv(rows, BF16_ROWS * n_steps)
        n_blocks = rows // block_rows
        assert rows % block_rows == 0 and n_blocks <= n_steps

        def imap(*g, n_blocks=n_blocks):
            return (jnp.minimum(step_of(*g), n_blocks - 1), 0)

        in_specs.append(_layer_spec((block_rows, width), imap, src_layer))
        out_specs.append(pl.BlockSpec((block_rows, width), imap))
        out_shapes.append(jax.ShapeDtypeStruct((rows, width), jnp.bfloat16))
    return in_specs, out_specs, out_shapes


def _run_casts(cast_in, cast_out):
    for src, dst in zip(cast_in, cast_out):
        dst[...] = src[...].astype(jnp.bfloat16)


N_SLABS = 4
GATHER_STRIDE = 4


def _in_proj_a_kernel(x_ref, g_ref, w_ref, *refs, tn, n_casts):
    cast_in, refs = refs[:n_casts], refs[n_casts:]
    group_refs = refs[:A_GROUPS]
    qm_ref = refs[A_GROUPS]
    cast_out = refs[A_GROUPS + 1:A_GROUPS + 1 + n_casts]
    h_ref, slab_ref, part_ref = refs[A_GROUPS + 1 + n_casts:]
    tm = x_ref.shape[0]
    _run_casts(cast_in, cast_out)
    _rms_to(h_ref, x_ref, g_ref[...])
    n_slab = 0
    for c in reversed(range(w_ref.shape[1] // tn)):
        acc = jnp.dot(h_ref[...], w_ref[:, c * tn:(c + 1) * tn],
                      preferred_element_type=jnp.float32)
        for cc in range(tn // HEAD_DIM):
            col = c * (tn // HEAD_DIM) + cc
            slab = acc[:, cc * HEAD_DIM:(cc + 1) * HEAD_DIM]
            which, rest = divmod(col, A_HEADS)
            if which == 3:
                qm_ref[rest] = slab.astype(qm_ref.dtype)
                continue
            grp, head = divmod(rest, HEADS_PER_GROUP)
            dst = group_refs[grp]
            dilation = A_PATTERNS[grp][1]
            if dilation == 1:
                dst[which * HEADS_PER_GROUP + head, 0] = slab.astype(dst.dtype)
                continue
            buf = n_slab % N_SLABS
            n_slab += 1
            slab_ref[buf] = slab
            out_col = which * HEADS_PER_GROUP + head
            if dilation <= GATHER_STRIDE:
                for r in range(dilation):
                    rows = slab_ref[buf, pl.ds(r, tm // dilation, stride=dilation), :]
                    dst[out_col, r] = rows.astype(dst.dtype)
                continue
            outer = dilation // GATHER_STRIDE
            for r1 in range(GATHER_STRIDE):
                part_ref[buf, r1] = slab_ref[buf, pl.ds(r1, tm // GATHER_STRIDE,
                                                        stride=GATHER_STRIDE), :]
            for r1 in range(GATHER_STRIDE):
                for k in range(outer):
                    rows = part_ref[buf, r1, pl.ds(k, tm // dilation, stride=outer), :]
                    dst[out_col, r1 + GATHER_STRIDE * k] = rows.astype(dst.dtype)


def _in_proj_a(x, g_all, layer_g, w, casts, *, tm, tn):
    m, d = x.shape
    n = w.shape[1]
    batch = m // SEQ
    tiles_per_seq = SEQ // tm
    assert m % tm == 0 and SEQ % tm == 0 and n == A_IN and n % tn == 0 and tn % HEAD_DIM == 0
    out_specs, out_shapes = [], []
    for _, dilation in A_PATTERNS:
        assert tm % (dilation * BF16_ROWS) == 0
        out_specs.append(pl.BlockSpec(
            (3 * HEADS_PER_GROUP, None, dilation, tm // dilation, HEAD_DIM),
            lambda i: (0, i // tiles_per_seq, 0, i % tiles_per_seq, 0)))
        out_shapes.append(jax.ShapeDtypeStruct(
            (3 * HEADS_PER_GROUP, batch, dilation, SEQ // dilation, HEAD_DIM), jnp.bfloat16))
    out_specs.append(pl.BlockSpec((MEM_HEADS, tm, HEAD_DIM), lambda i: (0, i, 0)))
    out_shapes.append(jax.ShapeDtypeStruct((MEM_HEADS, m, HEAD_DIM), jnp.bfloat16))
    cast_in_specs, cast_out_specs, cast_out_shapes = _cast_specs(casts, m // tm, lambda i: i)
    outs = pl.pallas_call(
        functools.partial(_in_proj_a_kernel, tn=tn, n_casts=len(casts)),
        grid=(m // tm,),
        in_specs=[
            pl.BlockSpec((tm, d), lambda i: (i, 0)),
            _layer_spec((1, d), lambda i: (0, 0), layer_g),
            _resident(w.shape),
        ] + cast_in_specs,
        out_specs=out_specs + cast_out_specs,
        out_shape=out_shapes + cast_out_shapes,
        scratch_shapes=[pltpu.VMEM((tm, d), jnp.bfloat16),
                        pltpu.VMEM((N_SLABS, tm, HEAD_DIM), jnp.float32),
                        pltpu.VMEM((N_SLABS, GATHER_STRIDE, tm // GATHER_STRIDE, HEAD_DIM),
                                   jnp.float32)],
        compiler_params=_params(1),
        name="in_proj_a",
    )(x, g_all, w, *[c[0] for c in casts])
    return outs[:A_GROUPS], outs[A_GROUPS], outs[A_GROUPS + 1:]


def _mem_kv_kernel(x_ref, g_ref, w_ref, o_ref):
    h = _rms(x_ref[...], g_ref[...]).astype(jnp.bfloat16)
    w = w_ref[...].astype(jnp.bfloat16)
    o_ref[...] = jnp.dot(h, w, preferred_element_type=jnp.float32).astype(o_ref.dtype)


def _mem_kv(mems, g_all, w_all):
    m, d = mems.shape
    layers, _, n = w_all.shape
    return pl.pallas_call(
        _mem_kv_kernel,
        grid=(layers,),
        in_specs=[
            pl.BlockSpec((m, d), lambda l: (0, 0)),
            pl.BlockSpec((None, 1, d), lambda l: (l, 0, 0)),
            pl.BlockSpec((None, d, n), lambda l: (l, 0, 0)),
        ],
        out_specs=pl.BlockSpec((None, m, n), lambda l: (l, 0, 0)),
        out_shape=jax.ShapeDtypeStruct((layers, m, n), jnp.bfloat16),
        compiler_params=_params(1),
        name="mem_kv",
    )(mems, g_all, w_all)


def _group_geometry(grp):
    window, dilation = A_PATTERNS[grp]
    n_side = (window // 2) // dilation
    length = SEQ // dilation
    kw = min(length, QBLK + 2 * n_side)
    return dilation, n_side, length, kw


def _key_start(qi, n_side, length, kw):
    return min(max(qi * QBLK - n_side, 0), length - kw)


def _dilated_attn_kernel(slopes_ref, *refs, n_casts):
    qkv_refs, refs = refs[:3 * A_GROUPS], refs[3 * A_GROUPS:]
    cast_in, refs = refs[:n_casts], refs[n_casts:]
    tok_ref, refs = refs[0], refs[1:]
    cast_out, refs = refs[:n_casts], refs[n_casts:]
    o_scr, lse_scr, bias_scr = refs[:3]
    stage = refs[3:]
    head = pl.program_id(1)
    scale = HEAD_DIM ** -0.5
    _run_casts(cast_in, cast_out)

    for grp in range(A_GROUPS):
        dilation, n_side, length, kw = _group_geometry(grp)
        nblk = length // QBLK
        q_ref, k_ref, v_ref = qkv_refs[3 * grp:3 * grp + 3]
        vs_ref = stage[grp]
        slope = slopes_ref[grp, head]

        vs_ref[:, :, 0:HEAD_DIM] = v_ref[...]
        vs_ref[:, :, HEAD_DIM:2 * HEAD_DIM] = jnp.ones((dilation, length, HEAD_DIM), jnp.bfloat16)

        offsets = sorted({qi * QBLK - _key_start(qi, n_side, length, kw) for qi in range(nblk)})
        for t, off in enumerate(offsets):
            rel = (lax.broadcasted_iota(jnp.int32, (QBLK, kw), 1)
                   - lax.broadcasted_iota(jnp.int32, (QBLK, kw), 0) - off)
            dist = jnp.abs(rel)
            alibi = (-slope) * (dist * dilation).astype(jnp.float32)
            bias_scr[grp, t, :, 0:kw] = jnp.where(dist <= n_side, alibi, NEG)

        for r in range(dilation):
            for qi in range(nblk):
                k0 = _key_start(qi, n_side, length, kw)
                t = offsets.index(qi * QBLK - k0)
                q = q_ref[r, qi * QBLK:(qi + 1) * QBLK, :]
                k = k_ref[r, k0:k0 + kw, :]
                v1 = vs_ref[r, k0:k0 + kw, :]
                bias = bias_scr[grp, t, :, 0:kw]
                s = lax.dot_general(q, k, (((1,), (1,)), ((), ())),
                                    preferred_element_type=jnp.float32) * scale
                s = jnp.where(bias > 0.5 * NEG, s + bias, NEG)
                m = jnp.max(s, axis=-1, keepdims=True)
                p = jnp.exp(s - m).astype(jnp.bfloat16)
                ol = jnp.dot(p, v1, preferred_element_type=jnp.float32)
                l = ol[:, HEAD_DIM:]
                start = qi * QBLK * dilation + r
                dst = pl.ds(start, QBLK, stride=dilation) if dilation > 1 else pl.ds(start, QBLK)
                o_scr[grp, dst, :] = ol[:, :HEAD_DIM] / l
                lse_scr[grp, dst, :] = m + jnp.log(l)

    rows_per_step = 256
    for c in range(SEQ // rows_per_step):
        rows = slice(c * rows_per_step, (c + 1) * rows_per_step)
        lses = [lse_scr[grp, rows, :] for grp in range(A_GROUPS)]
        mx = functools.reduce(jnp.maximum, lses)
        es = [jnp.exp(l - mx) for l in lses]
        den = functools.reduce(lambda a, b: a + b, es)
        tok = sum((e / den) * o_scr[grp, rows, :] for grp, e in enumerate(es))
        tok_ref[rows, :] = tok.astype(tok_ref.dtype)


def _dilated_attention(qkv, casts, *, batch):
    slopes = jnp.asarray(_alibi_slopes().reshape(A_GROUPS, HEADS_PER_GROUP))
    in_specs, operands, stage = [], [], []
    for grp in range(A_GROUPS):
        dilation, _, length, _ = _group_geometry(grp)
        for which in range(3):
            in_specs.append(pl.BlockSpec(
                (None, None, dilation, length, HEAD_DIM),
                lambda b, h, sl, which=which: (which * HEADS_PER_GROUP + h, b, 0, 0, 0)))
            operands.append(qkv[grp])
        stage.append(pltpu.VMEM((dilation, length, 2 * HEAD_DIM), jnp.bfloat16))
    max_kw = max(_group_geometry(grp)[3] for grp in range(A_GROUPS))
    cast_in_specs, cast_out_specs, cast_out_shapes = _cast_specs(
        casts, batch * HEADS_PER_GROUP, lambda b, h, sl: b * HEADS_PER_GROUP + h)
    outs = pl.pallas_call(
        functools.partial(_dilated_attn_kernel, n_casts=len(casts)),
        grid_spec=pltpu.PrefetchScalarGridSpec(
            num_scalar_prefetch=1,
            grid=(batch, HEADS_PER_GROUP),
            in_specs=in_specs + cast_in_specs,
            out_specs=[pl.BlockSpec((SEQ, HEAD_DIM), lambda b, h, sl: (b, h))] + cast_out_specs,
            scratch_shapes=[
                pltpu.VMEM((A_GROUPS, SEQ, HEAD_DIM), jnp.float32),
                pltpu.VMEM((A_GROUPS, SEQ, HEAD_DIM), jnp.float32),
                pltpu.VMEM((A_GROUPS, 3, QBLK, max_kw), jnp.float32),
            ] + stage,
        ),
        out_shape=[jax.ShapeDtypeStruct((batch * SEQ, GROUP_W), jnp.bfloat16)] + cast_out_shapes,
        compiler_params=_params(2),
        name="dilated_attn",
    )(slopes, *operands, *[c[0] for c in casts])
    return outs[0], outs[1:]


def _mem_attention(q_heads, kv_ref):
    scale = HEAD_DIM ** -0.5
    outs = []
    for h, q in enumerate(q_heads):
        k = kv_ref[:, h * HEAD_DIM:(h + 1) * HEAD_DIM]
        v = kv_ref[:, MEM_W + h * HEAD_DIM:MEM_W + (h + 1) * HEAD_DIM]
        s = lax.dot_general(q, k, (((1,), (1,)), ((), ())),
                            preferred_element_type=jnp.float32) * scale
        m = jnp.max(s, axis=-1, keepdims=True)
        p = jnp.exp(s - m)
        l = jnp.sum(p, axis=-1, keepdims=True)
        o = jnp.dot(p.astype(jnp.bfloat16), v, preferred_element_type=jnp.float32)
        outs.append(o / l)
    return jnp.concatenate(outs, axis=-1)


def _mix_out_a_kernel(x_ref, tok_ref, qm_ref, kv_ref, w_ref, out_ref):
    q_heads = [qm_ref[h] for h in range(MEM_HEADS)]
    mem_out = _mem_attention(q_heads, kv_ref).astype(jnp.bfloat16)
    cat = jnp.concatenate([tok_ref[...], mem_out], axis=-1)
    out_ref[...] = x_ref[...] + jnp.dot(cat, w_ref[...], preferred_element_type=jnp.float32)


def _kv_spec(layer, batch, tiles_per_seq):
    return pl.BlockSpec((None, MEM_LEN, 2 * MEM_W),
                        lambda i: (layer * batch + i // tiles_per_seq, 0, 0))


def _mix_out_a(x, tok, qm, kv, w_out, layer, *, tm):
    m, d = x.shape
    tiles_per_seq = SEQ // tm
    batch = m // SEQ
    row = lambda i: (i, 0)
    return pl.pallas_call(
        _mix_out_a_kernel,
        grid=(m // tm,),
        in_specs=[
            pl.BlockSpec((tm, d), row),
            pl.BlockSpec((tm, GROUP_W), row),
            pl.BlockSpec((MEM_HEADS, tm, HEAD_DIM), lambda i: (0, i, 0)),
            _kv_spec(layer, batch, tiles_per_seq),
            pl.BlockSpec(w_out.shape, lambda i: (0, 0)),
        ],
        out_specs=pl.BlockSpec((tm, d), row),
        out_shape=jax.ShapeDtypeStruct((m, d), jnp.float32),
        compiler_params=_params(1),
        name="mix_out_a",
    )(x, tok, qm, kv, w_out)


def _mixer_b_kernel(x_ref, g_ref, win_ref, vg_ref, ws_ref, sb_ref, kv_ref, wout_ref, out_ref,
                    h_ref, uvq_ref, tok_ref, *, tn):
    tm = x_ref.shape[0]
    _rms_to(h_ref, x_ref, g_ref[...])
    for c in range(B_IN // tn):
        acc = jnp.dot(h_ref[...], win_ref[:, c * tn:(c + 1) * tn],
                      preferred_element_type=jnp.float32)
        n_gelu = min(max(2 * B_W - c * tn, 0), tn)
        if n_gelu:
            uvq_ref[:, c * tn:c * tn + n_gelu] = _gelu(acc[:, :n_gelu])
        if n_gelu < tn:
            uvq_ref[:, c * tn + n_gelu:(c + 1) * tn] = acc[:, n_gelu:]

    vn = _rms(uvq_ref[:, B_W:2 * B_W], vg_ref[...]).astype(jnp.bfloat16)
    n_chunks = tm // CHUNK
    for g in range(B_GROUPS):
        cols = slice(g * 128, (g + 1) * 128)
        v_g = jnp.concatenate([vn[c * CHUNK:(c + 1) * CHUNK, cols] for c in range(n_chunks)],
                              axis=1)
        mixed = jnp.dot(ws_ref[g], v_g, preferred_element_type=jnp.float32)
        mixed = mixed + sb_ref[:, g:g + 1]
        for c in range(n_chunks):
            rows = slice(c * CHUNK, (c + 1) * CHUNK)
            tok_ref[rows, cols] = (uvq_ref[rows, cols]
                                   * mixed[:, c * 128:(c + 1) * 128]).astype(jnp.bfloat16)
    qm = uvq_ref[:, 2 * B_W:].astype(jnp.bfloat16)
    q_heads = [qm[:, hd * HEAD_DIM:(hd + 1) * HEAD_DIM] for hd in range(MEM_HEADS)]
    mem_out = _mem_attention(q_heads, kv_ref).astype(jnp.bfloat16)
    cat = jnp.concatenate([tok_ref[...], mem_out], axis=-1)
    out_ref[...] = x_ref[...] + jnp.dot(cat, wout_ref[...], preferred_element_type=jnp.float32)


def _mixer_b(x, g_all, w_in, kv, v_norm_g_all, w_s_all, s_bias_t_all, w_out, layer, layer_b, *,
             tm, tn):
    m, d = x.shape
    assert m % tm == 0 and SEQ % tm == 0 and tm % CHUNK == 0 and B_IN % tn == 0
    tiles_per_seq = SEQ // tm
    batch = m // SEQ
    row = lambda i: (i, 0)
    const2 = lambda i: (0, 0)
    return pl.pallas_call(
        functools.partial(_mixer_b_kernel, tn=tn),
        grid=(m // tm,),
        in_specs=[
            pl.BlockSpec((tm, d), row),
            _layer_spec((1, d), const2, layer),
            _resident(w_in.shape),
            _layer_spec((1, B_W), const2, layer_b),
            _layer_spec(w_s_all.shape[1:], lambda i: (0, 0, 0), layer_b),
            _layer_spec(s_bias_t_all.shape[1:], const2, layer_b),
            _kv_spec(layer, batch, tiles_per_seq),
            _resident(w_out.shape),
        ],
        out_specs=pl.BlockSpec((tm, d), row),
        out_shape=jax.ShapeDtypeStruct((m, d), jnp.float32),
        scratch_shapes=[
            pltpu.VMEM((tm, d), jnp.bfloat16),
            pltpu.VMEM((tm, B_IN), jnp.float32),
            pltpu.VMEM((tm, B_W), jnp.bfloat16),
        ],
        compiler_params=_params(1),
        name="mixer_b",
    )(x, g_all, w_in, v_norm_g_all, w_s_all, s_bias_t_all, kv, w_out)


HALO = BF16_ROWS
FFN_TM = 1024
FFN_TF = 512
MXU_N = 256


N_FFN_IN = 9
CONV_TAPS = 3


def _conv_ffn_kernel(*refs, n_casts, tiles_per_seq, final_norm):
    (x_hbm, xp_ref, xn_ref, g_ref, wg_ref, wv_ref, cp_ref, wd_ref,
     fg_ref) = refs[:N_FFN_IN]
    cast_in = refs[N_FFN_IN:N_FFN_IN + n_casts]
    out_ref = refs[N_FFN_IN + n_casts]
    cast_out = refs[N_FFN_IN + n_casts + 1:N_FFN_IN + 2 * n_casts + 1]
    h_ref, ag_ref, av_ref, x_ref, x_sem = refs[N_FFN_IN + 2 * n_casts + 1:]
    i = pl.program_id(0)
    j = pl.program_id(1)
    n_tiles = pl.num_programs(0)
    nj = pl.num_programs(1)
    tm = x_ref.shape[0]

    def x_copy(tile):
        return pltpu.make_async_copy(x_hbm.at[pl.ds(tile * tm, tm), :], x_ref, x_sem)

    @pl.when(jnp.logical_and(i == 0, j == 0))
    def _():
        x_copy(0).start()

    @pl.when(j == 0)
    def _():
        x_copy(i).wait()
        g = g_ref[...]
        _rms_to(h_ref, x_ref, g, copy_ref=out_ref)
        first = (i % tiles_per_seq) == 0
        last = (i % tiles_per_seq) == tiles_per_seq - 1
        r = lax.broadcasted_iota(jnp.int32, (HALO, 1), 0)
        take_next = jnp.logical_and(r == 0, jnp.logical_not(last))
        take_prev = jnp.logical_and(r == HALO - 1, jnp.logical_not(first))
        halo = jnp.where(take_next, _rms(xn_ref[...], g),
                         jnp.where(take_prev, _rms(xp_ref[...], g), 0.0))
        h_ref[tm:tm + HALO, :] = halo.astype(jnp.bfloat16)

    @pl.when(jnp.logical_and(j == 1, i + 1 < n_tiles))
    def _():
        x_copy(i + 1).start()

    _run_casts(cast_in, cast_out)

    def up(w_ref, a_ref):
        a = jnp.dot(h_ref[...], w_ref[...], preferred_element_type=jnp.float32)
        a_ref[HALO:HALO + tm, :] = a[0:tm]
        a_ref[0:HALO, :] = a[tm:tm + HALO]
        a_ref[HALO + tm:2 * HALO + tm, :] = a[tm:tm + HALO]

    def conv(a_ref, half, cols):
        cp = cp_ref[half]
        return (a_ref[HALO - 1:HALO - 1 + tm, cols] * cp[0:1, cols]
                + a_ref[HALO:HALO + tm, cols] * cp[1:2, cols]
                + a_ref[HALO + 1:HALO + 1 + tm, cols] * cp[2:3, cols]
                + cp[CONV_TAPS:CONV_TAPS + 1, cols])

    up(wg_ref, ag_ref)
    up(wv_ref, av_ref)
    tf = wg_ref.shape[1]
    upd = None
    for c in range(tf // MXU_N):
        cols = slice(c * MXU_N, (c + 1) * MXU_N)
        act = (_gelu(conv(ag_ref, 0, cols)) * conv(av_ref, 1, cols)).astype(jnp.bfloat16)
        part = jnp.dot(act, wd_ref[cols, :], preferred_element_type=jnp.float32)
        upd = part if upd is None else upd + part
    out_ref[...] += upd

    if final_norm:
        @pl.when(j == nj - 1)
        def _():
            _rms_to(out_ref, out_ref, fg_ref[...])


def _conv_ffn(x, g_all, w_up, w_down, cp_all, final_g, layer, casts, *, tm, tf, final_norm):
    m, d = x.shape
    nf = FF // tf
    assert m % tm == 0 and FF % tf == 0 and SEQ % tm == 0 and tm % HALO == 0
    assert nf >= 2
    n_tiles = m // tm
    n_steps = n_tiles * nf
    tiles_per_seq = SEQ // tm
    hb = tm // HALO
    n_hblocks = m // HALO

    cast_in_specs, cast_out_specs, cast_out_shapes = _cast_specs(
        casts, n_steps, lambda i, j: i * nf + j)

    outs = pl.pallas_call(
        functools.partial(_conv_ffn_kernel, n_casts=len(casts),
                          tiles_per_seq=tiles_per_seq, final_norm=final_norm),
        grid=(n_tiles, nf),
        in_specs=[
            pl.BlockSpec(memory_space=pl.ANY),
            pl.BlockSpec((HALO, d), lambda i, j: (jnp.maximum(i * hb - 1, 0), 0)),
            pl.BlockSpec((HALO, d), lambda i, j: (jnp.minimum((i + 1) * hb, n_hblocks - 1), 0)),
            _layer_spec((1, d), lambda i, j: (0, 0), layer),
            pl.BlockSpec((d, tf), lambda i, j: (0, j)),
            pl.BlockSpec((d, tf), lambda i, j: (0, nf + j)),
            _layer_spec((2, CONV_TAPS + 1, tf), lambda i, j: (0, 0, j), layer),
            pl.BlockSpec((tf, d), lambda i, j: (j, 0)),
            pl.BlockSpec((1, d), lambda i, j: (0, 0)),
        ] + cast_in_specs,
        out_specs=[pl.BlockSpec((tm, d), lambda i, j: (i, 0))] + cast_out_specs,
        out_shape=[jax.ShapeDtypeStruct((m, d), jnp.float32)] + cast_out_shapes,
        scratch_shapes=[
            pltpu.VMEM((tm + HALO, d), jnp.bfloat16),
            pltpu.VMEM((tm + 2 * HALO, tf), jnp.float32),
            pltpu.VMEM((tm + 2 * HALO, tf), jnp.float32),
            pltpu.VMEM((tm, d), jnp.float32),
            pltpu.SemaphoreType.DMA(()),
        ],
        compiler_params=_params(2),
        name="conv_ffn",
    )(x, x, x, g_all, w_up, w_up, cp_all, w_down, final_g, *[c[0] for c in casts])
    return outs[0], outs[1:]


def kernel(x, mem, mix_norm_g, ffn_norm_g, mem_norm_g, w_mem_kv, a_w_in, a_w_out, b_w_in,
           b_v_norm_g, b_w_s, b_s_bias, b_w_out, ffn_w_up, ffn_conv_w, ffn_conv_b, ffn_w_down,
           final_norm_g):
    batch, seq, d = x.shape
    assert (seq, d) == (SEQ, D_MODEL)
    bf = jnp.bfloat16
    xs = x.reshape(batch * seq, d)
    mems = mem.reshape(batch * MEM_LEN, d)

    mix_g = mix_norm_g.reshape(DEPTH, 1, d)
    ffn_g = ffn_norm_g.reshape(DEPTH, 1, d)
    mem_g = mem_norm_g.reshape(DEPTH, 1, d)
    final_g = final_norm_g.reshape(1, d)
    v_norm_g = b_v_norm_g.reshape(-1, 1, B_W)
    s_bias_t = jnp.swapaxes(b_s_bias, 1, 2)
    conv_p = jnp.concatenate([ffn_conv_w.reshape(DEPTH, CONV_TAPS, 2, FF).transpose(0, 2, 1, 3),
                              ffn_conv_b.reshape(DEPTH, 2, 1, FF)], axis=2)
    b_w_s = b_w_s.astype(bf)
    kv = _mem_kv(mems, mem_g, w_mem_kv).reshape(DEPTH * batch, MEM_LEN, 2 * MEM_W)

    def mixer_weights(i):
        return (a_w_in, a_w_out) if i % 2 == 0 else (b_w_in, b_w_out)

    w_in = a_w_in[0].astype(bf)
    w_out = w_up = w_down = None

    for i in range(DEPTH):
        j = i // 2
        if i % 2 == 0:
            first = i == 0
            qkv, qm, cast = _in_proj_a(xs, mix_g, i, w_in,
                                       [(a_w_out, 0), (ffn_w_down, 0)] if first else [],
                                       tm=512, tn=1280)
            if first:
                w_out, w_down = cast
            tok, cast = _dilated_attention(qkv, [(ffn_w_up, 0)] if first else [], batch=batch)
            if first:
                w_up, = cast
            xs = _mix_out_a(xs, tok, qm, kv, w_out, i, tm=1024)
        else:
            xs = _mixer_b(xs, mix_g, w_in, kv, v_norm_g, b_w_s, s_bias_t, w_out, i, j,
                          tm=512, tn=1792)
        casts = []
        if i + 1 < DEPTH:
            nxt_in, nxt_out = mixer_weights(i + 1)
            casts = [(nxt_in, (i + 1) // 2), (nxt_out, (i + 1) // 2), (ffn_w_up, i + 1),
                     (ffn_w_down, i + 1)]
        xs, nxt = _conv_ffn(xs, ffn_g, w_up, w_down, conv_p, final_g, i, casts,
                            tm=FFN_TM, tf=FFN_TF, final_norm=(i == DEPTH - 1))
        if nxt:
            w_in, w_out, w_up, w_down = nxt
    return xs.reshape(batch, seq, d)
```

```python
import functools

import numpy as np
import jax
import jax.numpy as jnp
from jax import lax
from jax.experimental import pallas as pl
from jax.experimental.pallas import tpu as pltpu

D_MODEL = 2048
SEQ = 2048
DEPTH = 4
EPS = 1e-6
NEG = -1e30

HEAD_DIM = 128
HEADS_PER_GROUP = 4
A_PATTERNS = ((128, 1), (512, 4), (2048, 16))
A_GROUPS = len(A_PATTERNS)
A_HEADS = HEADS_PER_GROUP * A_GROUPS
A_QKV_W = A_HEADS * HEAD_DIM
GROUP_W = HEADS_PER_GROUP * HEAD_DIM
QBLK = 128

CHUNK = 128
B_GROUPS = 12
B_W = B_GROUPS * 128

MEM_LEN = 256
MEM_HEADS = 4
MEM_W = MEM_HEADS * HEAD_DIM

A_IN = 3 * A_QKV_W + MEM_W
B_IN = 2 * B_W + MEM_W
FF = 5632

VMEM_LIMIT_BYTES = 62 * 1024 * 1024

_SQRT_HALF = 0.7071067811865476


def _params(n_axes):
    return pltpu.CompilerParams(
        dimension_semantics=("arbitrary",) * n_axes,
        vmem_limit_bytes=VMEM_LIMIT_BYTES,
    )


def _rms(x, g):
    y = x * lax.rsqrt(jnp.mean(x * x, axis=-1, keepdims=True) + EPS)
    return y * g


NORM_ROWS = 16


def _rms_to(dst_ref, x_ref, g, copy_ref=None):
    n = x_ref.shape[0]
    for r0 in range(0, n, NORM_ROWS):
        rows = slice(r0, r0 + NORM_ROWS)
        x = x_ref[rows, :]
        dst_ref[rows, :] = _rms(x, g).astype(dst_ref.dtype)
        if copy_ref is not None:
            copy_ref[rows, :] = x


def _gelu(x):
    return 0.5 * x * (1.0 + lax.erf(x * _SQRT_HALF))


def _alibi_slopes():
    return (2.0 ** (-8.0 * (np.arange(A_HEADS) + 1) / A_HEADS)).astype(np.float32)


def _layer_spec(block, index_map, layer):
    return pl.BlockSpec((None,) + block, lambda *g: (layer,) + index_map(*g))


def _resident(shape):
    return pl.BlockSpec(shape, lambda *g: (0,) * len(shape), pipeline_mode=pl.Buffered(1))


BF16_ROWS = 16


def _cast_specs(casts, n_steps, step_of):
    in_specs, out_specs, out_shapes = [], [], []
    for src, src_layer in casts:
        _, rows, width = src.shape
        block_rows = BF16_ROWS * pl.cdiv(rows, BF16_ROWS * n_steps)
        n_blocks = rows // block_rows
        assert rows % block_rows == 0 and n_blocks <= n_steps

        def imap(*g, n_blocks=n_blocks):
            return (jnp.minimum(step_of(*g), n_blocks - 1), 0)

        in_specs.append(_layer_spec((block_rows, width), imap, src_layer))
        out_specs.append(pl.BlockSpec((block_rows, width), imap))
        out_shapes.append(jax.ShapeDtypeStruct((rows, width), jnp.bfloat16))
    return in_specs, out_specs, out_shapes


def _run_casts(cast_in, cast_out):
    for src, dst in zip(cast_in, cast_out):
        dst[...] = src[...].astype(jnp.bfloat16)


N_SLABS = 4
GATHER_STRIDE = 4


def _in_proj_a_kernel(x_ref, g_ref, w_ref, *refs, tn, n_casts):
    cast_in, refs = refs[:n_casts], refs[n_casts:]
    group_refs = refs[:A_GROUPS]
    qm_ref = refs[A_GROUPS]
    cast_out = refs[A_GROUPS + 1:A_GROUPS + 1 + n_casts]
    h_ref, slab_ref, part_ref = refs[A_GROUPS + 1 + n_casts:]
    tm = x_ref.shape[0]
    _run_casts(cast_in, cast_out)
    _rms_to(h_ref, x_ref, g_ref[...])
    n_slab = 0
    for c in reversed(range(w_ref.shape[1] // tn)):
        acc = jnp.dot(h_ref[...], w_ref[:, c * tn:(c + 1) * tn],
                      preferred_element_type=jnp.float32)
        for cc in range(tn // HEAD_DIM):
            col = c * (tn // HEAD_DIM) + cc
            slab = acc[:, cc * HEAD_DIM:(cc + 1) * HEAD_DIM]
            which, rest = divmod(col, A_HEADS)
            if which == 3:
                qm_ref[rest] = slab.astype(qm_ref.dtype)
                continue
            grp, head = divmod(rest, HEADS_PER_GROUP)
            dst = group_refs[grp]
            dilation = A_PATTERNS[grp][1]
            if dilation == 1:
                dst[which * HEADS_PER_GROUP + head, 0] = slab.astype(dst.dtype)
                continue
            buf = n_slab % N_SLABS
            n_slab += 1
            slab_ref[buf] = slab
            out_col = which * HEADS_PER_GROUP + head
            if dilation <= GATHER_STRIDE:
                for r in range(dilation):
                    rows = slab_ref[buf, pl.ds(r, tm // dilation, stride=dilation), :]
                    dst[out_col, r] = rows.astype(dst.dtype)
                continue
            outer = dilation // GATHER_STRIDE
            for r1 in range(GATHER_STRIDE):
                part_ref[buf, r1] = slab_ref[buf, pl.ds(r1, tm // GATHER_STRIDE,
                                                        stride=GATHER_STRIDE), :]
            for r1 in range(GATHER_STRIDE):
                for k in range(outer):
                    rows = part_ref[buf, r1, pl.ds(k, tm // dilation, stride=outer), :]
                    dst[out_col, r1 + GATHER_STRIDE * k] = rows.astype(dst.dtype)


def _in_proj_a(x, g_all, layer_g, w, casts, *, tm, tn):
    m, d = x.shape
    n = w.shape[1]
    batch = m // SEQ
    tiles_per_seq = SEQ // tm
    assert m % tm == 0 and SEQ % tm == 0 and n == A_IN and n % tn == 0 and tn % HEAD_DIM == 0
    out_specs, out_shapes = [], []
    for _, dilation in A_PATTERNS:
        assert tm % (dilation * BF16_ROWS) == 0
        out_specs.append(pl.BlockSpec(
            (3 * HEADS_PER_GROUP, None, dilation, tm // dilation, HEAD_DIM),
            lambda i: (0, i // tiles_per_seq, 0, i % tiles_per_seq, 0)))
        out_shapes.append(jax.ShapeDtypeStruct(
            (3 * HEADS_PER_GROUP, batch, dilation, SEQ // dilation, HEAD_DIM), jnp.bfloat16))
    out_specs.append(pl.BlockSpec((MEM_HEADS, tm, HEAD_DIM), lambda i: (0, i, 0)))
    out_shapes.append(jax.ShapeDtypeStruct((MEM_HEADS, m, HEAD_DIM), jnp.bfloat16))
    cast_in_specs, cast_out_specs, cast_out_shapes = _cast_specs(casts, m // tm, lambda i: i)
    outs = pl.pallas_call(
        functools.partial(_in_proj_a_kernel, tn=tn, n_casts=len(casts)),
        grid=(m // tm,),
        in_specs=[
            pl.BlockSpec((tm, d), lambda i: (i, 0)),
            _layer_spec((1, d), lambda i: (0, 0), layer_g),
            _resident(w.shape),
        ] + cast_in_specs,
        out_specs=out_specs + cast_out_specs,
        out_shape=out_shapes + cast_out_shapes,
        scratch_shapes=[pltpu.VMEM((tm, d), jnp.bfloat16),
                        pltpu.VMEM((N_SLABS, tm, HEAD_DIM), jnp.float32),
                        pltpu.VMEM((N_SLABS, GATHER_STRIDE, tm // GATHER_STRIDE, HEAD_DIM),
                                   jnp.float32)],
        compiler_params=_params(1),
        name="in_proj_a",
    )(x, g_all, w, *[c[0] for c in casts])
    return outs[:A_GROUPS], outs[A_GROUPS], outs[A_GROUPS + 1:]


def _mem_kv_kernel(x_ref, g_ref, w_ref, o_ref):
    h = _rms(x_ref[...], g_ref[...]).astype(jnp.bfloat16)
    w = w_ref[...].astype(jnp.bfloat16)
    o_ref[...] = jnp.dot(h, w, preferred_element_type=jnp.float32).astype(o_ref.dtype)


def _mem_kv(mems, g_all, w_all):
    m, d = mems.shape
    layers, _, n = w_all.shape
    return pl.pallas_call(
        _mem_kv_kernel,
        grid=(layers,),
        in_specs=[
            pl.BlockSpec((m, d), lambda l: (0, 0)),
            pl.BlockSpec((None, 1, d), lambda l: (l, 0, 0)),
            pl.BlockSpec((None, d, n), lambda l: (l, 0, 0)),
        ],
        out_specs=pl.BlockSpec((None, m, n), lambda l: (l, 0, 0)),
        out_shape=jax.ShapeDtypeStruct((layers, m, n), jnp.bfloat16),
        compiler_params=_params(1),
        name="mem_kv",
    )(mems, g_all, w_all)


def _group_geometry(grp):
    window, dilation = A_PATTERNS[grp]
    n_side = (window // 2) // dilation
    length = SEQ // dilation
    kw = min(length, QBLK + 2 * n_side)
    return dilation, n_side, length, kw


def _key_start(qi, n_side, length, kw):
    return min(max(qi * QBLK - n_side, 0), length - kw)


def _dilated_attn_kernel(slopes_ref, *refs, n_casts):
    qkv_refs, refs = refs[:3 * A_GROUPS], refs[3 * A_GROUPS:]
    cast_in, refs = refs[:n_casts], refs[n_casts:]
    tok_ref, refs = refs[0], refs[1:]
    cast_out, refs = refs[:n_casts], refs[n_casts:]
    o_scr, lse_scr, bias_scr = refs[:3]
    stage = refs[3:]
    head = pl.program_id(1)
    scale = HEAD_DIM ** -0.5
    _run_casts(cast_in, cast_out)

    for grp in range(A_GROUPS):
        dilation, n_side, length, kw = _group_geometry(grp)
        nblk = length // QBLK
        q_ref, k_ref, v_ref = qkv_refs[3 * grp:3 * grp + 3]
        vs_ref = stage[grp]
        slope = slopes_ref[grp, head]

        vs_ref[:, :, 0:HEAD_DIM] = v_ref[...]
        vs_ref[:, :, HEAD_DIM:2 * HEAD_DIM] = jnp.ones((dilation, length, HEAD_DIM), jnp.bfloat16)

        offsets = sorted({qi * QBLK - _key_start(qi, n_side, length, kw) for qi in range(nblk)})
        for t, off in enumerate(offsets):
            rel = (lax.broadcasted_iota(jnp.int32, (QBLK, kw), 1)
                   - lax.broadcasted_iota(jnp.int32, (QBLK, kw), 0) - off)
            dist = jnp.abs(rel)
            alibi = (-slope) * (dist * dilation).astype(jnp.float32)
            bias_scr[grp, t, :, 0:kw] = jnp.where(dist <= n_side, alibi, NEG)

        for r in range(dilation):
            for qi in range(nblk):
                k0 = _key_start(qi, n_side, length, kw)
                t = offsets.index(qi * QBLK - k0)
                q = q_ref[r, qi * QBLK:(qi + 1) * QBLK, :]
                k = k_ref[r, k0:k0 + kw, :]
                v1 = vs_ref[r, k0:k0 + kw, :]
                bias = bias_scr[grp, t, :, 0:kw]
                s = lax.dot_general(q, k, (((1,), (1,)), ((), ())),
                                    preferred_element_type=jnp.float32) * scale
                s = jnp.where(bias > 0.5 * NEG, s + bias, NEG)
                m = jnp.max(s, axis=-1, keepdims=True)
                p = jnp.exp(s - m).astype(jnp.bfloat16)
                ol = jnp.dot(p, v1, preferred_element_type=jnp.float32)
                l = ol[:, HEAD_DIM:]
                start = qi * QBLK * dilation + r
                dst = pl.ds(start, QBLK, stride=dilation) if dilation > 1 else pl.ds(start, QBLK)
                o_scr[grp, dst, :] = ol[:, :HEAD_DIM] / l
                lse_scr[grp, dst, :] = m + jnp.log(l)

    rows_per_step = 256
    for c in range(SEQ // rows_per_step):
        rows = slice(c * rows_per_step, (c + 1) * rows_per_step)
        lses = [lse_scr[grp, rows, :] for grp in range(A_GROUPS)]
        mx = functools.reduce(jnp.maximum, lses)
        es = [jnp.exp(l - mx) for l in lses]
        den = functools.reduce(lambda a, b: a + b, es)
        tok = sum((e / den) * o_scr[grp, rows, :] for grp, e in enumerate(es))
        tok_ref[rows, :] = tok.astype(tok_ref.dtype)


def _dilated_attention(qkv, casts, *, batch):
    slopes = jnp.asarray(_alibi_slopes().reshape(A_GROUPS, HEADS_PER_GROUP))
    in_specs, operands, stage = [], [], []
    for grp in range(A_GROUPS):
        dilation, _, length, _ = _group_geometry(grp)
        for which in range(3):
            in_specs.append(pl.BlockSpec(
                (None, None, dilation, length, HEAD_DIM),
                lambda b, h, sl, which=which: (which * HEADS_PER_GROUP + h, b, 0, 0, 0)))
            operands.append(qkv[grp])
        stage.append(pltpu.VMEM((dilation, length, 2 * HEAD_DIM), jnp.bfloat16))
    max_kw = max(_group_geometry(grp)[3] for grp in range(A_GROUPS))
    cast_in_specs, cast_out_specs, cast_out_shapes = _cast_specs(
        casts, batch * HEADS_PER_GROUP, lambda b, h, sl: b * HEADS_PER_GROUP + h)
    outs = pl.pallas_call(
        functools.partial(_dilated_attn_kernel, n_casts=len(casts)),
        grid_spec=pltpu.PrefetchScalarGridSpec(
            num_scalar_prefetch=1,
            grid=(batch, HEADS_PER_GROUP),
            in_specs=in_specs + cast_in_specs,
            out_specs=[pl.BlockSpec((SEQ, HEAD_DIM), lambda b, h, sl: (b, h))] + cast_out_specs,
            scratch_shapes=[
                pltpu.VMEM((A_GROUPS, SEQ, HEAD_DIM), jnp.float32),
                pltpu.VMEM((A_GROUPS, SEQ, HEAD_DIM), jnp.float32),
                pltpu.VMEM((A_GROUPS, 3, QBLK, max_kw), jnp.float32),
            ] + stage,
        ),
        out_shape=[jax.ShapeDtypeStruct((batch * SEQ, GROUP_W), jnp.bfloat16)] + cast_out_shapes,
        compiler_params=_params(2),
        name="dilated_attn",
    )(slopes, *operands, *[c[0] for c in casts])
    return outs[0], outs[1:]


def _mem_attention(q_heads, kv_ref):
    scale = HEAD_DIM ** -0.5
    outs = []
    for h, q in enumerate(q_heads):
        k = kv_ref[:, h * HEAD_DIM:(h + 1) * HEAD_DIM]
        v = kv_ref[:, MEM_W + h * HEAD_DIM:MEM_W + (h + 1) * HEAD_DIM]
        s = lax.dot_general(q, k, (((1,), (1,)), ((), ())),
                            preferred_element_type=jnp.float32) * scale
        m = jnp.max(s, axis=-1, keepdims=True)
        p = jnp.exp(s - m)
        l = jnp.sum(p, axis=-1, keepdims=True)
        o = jnp.dot(p.astype(jnp.bfloat16), v, preferred_element_type=jnp.float32)
        outs.append(o / l)
    return jnp.concatenate(outs, axis=-1)


def _mix_out_a_kernel(x_ref, tok_ref, qm_ref, kv_ref, w_ref, out_ref):
    q_heads = [qm_ref[h] for h in range(MEM_HEADS)]
    mem_out = _mem_attention(q_heads, kv_ref).astype(jnp.bfloat16)
    cat = jnp.concatenate([tok_ref[...], mem_out], axis=-1)
    out_ref[...] = x_ref[...] + jnp.dot(cat, w_ref[...], preferred_element_type=jnp.float32)


def _kv_spec(layer, batch, tiles_per_seq):
    return pl.BlockSpec((None, MEM_LEN, 2 * MEM_W),
                        lambda i: (layer * batch + i // tiles_per_seq, 0, 0))


def _mix_out_a(x, tok, qm, kv, w_out, layer, *, tm):
    m, d = x.shape
    tiles_per_seq = SEQ // tm
    batch = m // SEQ
    row = lambda i: (i, 0)
    return pl.pallas_call(
        _mix_out_a_kernel,
        grid=(m // tm,),
        in_specs=[
            pl.BlockSpec((tm, d), row),
            pl.BlockSpec((tm, GROUP_W), row),
            pl.BlockSpec((MEM_HEADS, tm, HEAD_DIM), lambda i: (0, i, 0)),
            _kv_spec(layer, batch, tiles_per_seq),
            pl.BlockSpec(w_out.shape, lambda i: (0, 0)),
        ],
        out_specs=pl.BlockSpec((tm, d), row),
        out_shape=jax.ShapeDtypeStruct((m, d), jnp.float32),
        compiler_params=_params(1),
        name="mix_out_a",
    )(x, tok, qm, kv, w_out)


def _mixer_b_kernel(x_ref, g_ref, win_ref, vg_ref, ws_ref, sb_ref, kv_ref, wout_ref, out_ref,
                    h_ref, uvq_ref, tok_ref, *, tn):
    tm = x_ref.shape[0]
    _rms_to(h_ref, x_ref, g_ref[...])
    for c in range(B_IN // tn):
        acc = jnp.dot(h_ref[...], win_ref[:, c * tn:(c + 1) * tn],
                      preferred_element_type=jnp.float32)
        n_gelu = min(max(2 * B_W - c * tn, 0), tn)
        if n_gelu:
            uvq_ref[:, c * tn:c * tn + n_gelu] = _gelu(acc[:, :n_gelu])
        if n_gelu < tn:
            uvq_ref[:, c * tn + n_gelu:(c + 1) * tn] = acc[:, n_gelu:]

    qm = uvq_ref[:, 2 * B_W:].astype(jnp.bfloat16)
    q_heads = [qm[:, hd * HEAD_DIM:(hd + 1) * HEAD_DIM] for hd in range(MEM_HEADS)]
    mem_out = _mem_attention(q_heads, kv_ref).astype(jnp.bfloat16)
    out_ref[...] = x_ref[...] + jnp.dot(mem_out, wout_ref[B_W:, :],
                                        preferred_element_type=jnp.float32)

    vn = _rms(uvq_ref[:, B_W:2 * B_W], vg_ref[...]).astype(jnp.bfloat16)
    n_chunks = tm // CHUNK
    n_parts = 3
    per_part = B_GROUPS // n_parts
    for part in range(n_parts):
        for g in range(part * per_part, (part + 1) * per_part):
            cols = slice(g * 128, (g + 1) * 128)
            v_g = jnp.concatenate([vn[c * CHUNK:(c + 1) * CHUNK, cols] for c in range(n_chunks)],
                                  axis=1)
            mixed = jnp.dot(ws_ref[g], v_g, preferred_element_type=jnp.float32)
            mixed = mixed + sb_ref[:, g:g + 1]
            for c in range(n_chunks):
                rows = slice(c * CHUNK, (c + 1) * CHUNK)
                tok_ref[rows, cols] = (uvq_ref[rows, cols]
                                       * mixed[:, c * 128:(c + 1) * 128]).astype(jnp.bfloat16)
        pcols = slice(part * per_part * 128, (part + 1) * per_part * 128)
        out_ref[...] += jnp.dot(tok_ref[:, pcols], wout_ref[pcols, :],
                                preferred_element_type=jnp.float32)


def _mixer_b(x, g_all, w_in, kv, v_norm_g_all, w_s_all, s_bias_t_all, w_out, layer, layer_b, *,
             tm, tn):
    m, d = x.shape
    assert m % tm == 0 and SEQ % tm == 0 and tm % CHUNK == 0 and B_IN % tn == 0
    tiles_per_seq = SEQ // tm
    batch = m // SEQ
    row = lambda i: (i, 0)
    const2 = lambda i: (0, 0)
    return pl.pallas_call(
        functools.partial(_mixer_b_kernel, tn=tn),
        grid=(m // tm,),
        in_specs=[
            pl.BlockSpec((tm, d), row),
            _layer_spec((1, d), const2, layer),
            _resident(w_in.shape),
            _layer_spec((1, B_W), const2, layer_b),
            _layer_spec(w_s_all.shape[1:], lambda i: (0, 0, 0), layer_b),
            _layer_spec(s_bias_t_all.shape[1:], const2, layer_b),
            _kv_spec(layer, batch, tiles_per_seq),
            _resident(w_out.shape),
        ],
        out_specs=pl.BlockSpec((tm, d), row),
        out_shape=jax.ShapeDtypeStruct((m, d), jnp.float32),
        scratch_shapes=[
            pltpu.VMEM((tm, d), jnp.bfloat16),
            pltpu.VMEM((tm, B_IN), jnp.float32),
            pltpu.VMEM((tm, B_W), jnp.bfloat16),
        ],
        compiler_params=_params(1),
        name="mixer_b",
    )(x, g_all, w_in, v_norm_g_all, w_s_all, s_bias_t_all, kv, w_out)


HALO = BF16_ROWS
FFN_TM = 1024
FFN_TF = 512
MXU_N = 256


N_FFN_IN = 9
CONV_TAPS = 3


def _conv_ffn_kernel(*refs, n_casts, tiles_per_seq, final_norm):
    (x_hbm, xp_ref, xn_ref, g_ref, wg_ref, wv_ref, cp_ref, wd_ref,
     fg_ref) = refs[:N_FFN_IN]
    cast_in = refs[N_FFN_IN:N_FFN_IN + n_casts]
    out_ref = refs[N_FFN_IN + n_casts]
    cast_out = refs[N_FFN_IN + n_casts + 1:N_FFN_IN + 2 * n_casts + 1]
    h_ref, ag_ref, av_ref, x_ref, x_sem = refs[N_FFN_IN + 2 * n_casts + 1:]
    i = pl.program_id(0)
    j = pl.program_id(1)
    n_tiles = pl.num_programs(0)
    nj = pl.num_programs(1)
    tm = x_ref.shape[0]

    def x_copy(tile):
        return pltpu.make_async_copy(x_hbm.at[pl.ds(tile * tm, tm), :], x_ref, x_sem)

    @pl.when(jnp.logical_and(i == 0, j == 0))
    def _():
        x_copy(0).start()

    @pl.when(j == 0)
    def _():
        x_copy(i).wait()
        g = g_ref[...]
        _rms_to(h_ref, x_ref, g, copy_ref=out_ref)
        first = (i % tiles_per_seq) == 0
        last = (i % tiles_per_seq) == tiles_per_seq - 1
        r = lax.broadcasted_iota(jnp.int32, (HALO, 1), 0)
        take_next = jnp.logical_and(r == 0, jnp.logical_not(last))
        take_prev = jnp.logical_and(r == HALO - 1, jnp.logical_not(first))
        halo = jnp.where(take_next, _rms(xn_ref[...], g),
                         jnp.where(take_prev, _rms(xp_ref[...], g), 0.0))
        h_ref[tm:tm + HALO, :] = halo.astype(jnp.bfloat16)

    @pl.when(jnp.logical_and(j == 1, i + 1 < n_tiles))
    def _():
        x_copy(i + 1).start()

    _run_casts(cast_in, cast_out)

    def up(w_ref, a_ref):
        a = jnp.dot(h_ref[...], w_ref[...], preferred_element_type=jnp.float32)
        a_ref[HALO:HALO + tm, :] = a[0:tm]
        a_ref[0:HALO, :] = a[tm:tm + HALO]
        a_ref[HALO + tm:2 * HALO + tm, :] = a[tm:tm + HALO]

    def conv(a_ref, half, cols):
        cp = cp_ref[half]
        return (a_ref[HALO - 1:HALO - 1 + tm, cols] * cp[0:1, cols]
                + a_ref[HALO:HALO + tm, cols] * cp[1:2, cols]
                + a_ref[HALO + 1:HALO + 1 + tm, cols] * cp[2:3, cols]
                + cp[CONV_TAPS:CONV_TAPS + 1, cols])

    up(wg_ref, ag_ref)
    up(wv_ref, av_ref)
    tf = wg_ref.shape[1]
    upd = None
    for c in range(tf // MXU_N):
        cols = slice(c * MXU_N, (c + 1) * MXU_N)
        act = (_gelu(conv(ag_ref, 0, cols)) * conv(av_ref, 1, cols)).astype(jnp.bfloat16)
        part = jnp.dot(act, wd_ref[cols, :], preferred_element_type=jnp.float32)
        upd = part if upd is None else upd + part
    out_ref[...] += upd

    if final_norm:
        @pl.when(j == nj - 1)
        def _():
            _rms_to(out_ref, out_ref, fg_ref[...])


def _conv_ffn(x, g_all, w_up, w_down, cp_all, final_g, layer, casts, *, tm, tf, final_norm):
    m, d = x.shape
    nf = FF // tf
    assert m % tm == 0 and FF % tf == 0 and SEQ % tm == 0 and tm % HALO == 0
    assert nf >= 2
    n_tiles = m // tm
    n_steps = n_tiles * nf
    tiles_per_seq = SEQ // tm
    hb = tm // HALO
    n_hblocks = m // HALO

    cast_in_specs, cast_out_specs, cast_out_shapes = _cast_specs(
        casts, n_steps, lambda i, j: i * nf + j)

    outs = pl.pallas_call(
        functools.partial(_conv_ffn_kernel, n_casts=len(casts),
                          tiles_per_seq=tiles_per_seq, final_norm=final_norm),
        grid=(n_tiles, nf),
        in_specs=[
            pl.BlockSpec(memory_space=pl.ANY),
            pl.BlockSpec((HALO, d), lambda i, j: (jnp.maximum(i * hb - 1, 0), 0)),
            pl.BlockSpec((HALO, d), lambda i, j: (jnp.minimum((i + 1) * hb, n_hblocks - 1), 0)),
            _layer_spec((1, d), lambda i, j: (0, 0), layer),
            pl.BlockSpec((d, tf), lambda i, j: (0, j)),
            pl.BlockSpec((d, tf), lambda i, j: (0, nf + j)),
            _layer_spec((2, CONV_TAPS + 1, tf), lambda i, j: (0, 0, j), layer),
            pl.BlockSpec((tf, d), lambda i, j: (j, 0)),
            pl.BlockSpec((1, d), lambda i, j: (0, 0)),
        ] + cast_in_specs,
        out_specs=[pl.BlockSpec((tm, d), lambda i, j: (i, 0))] + cast_out_specs,
        out_shape=[jax.ShapeDtypeStruct((m, d), jnp.float32)] + cast_out_shapes,
        scratch_shapes=[
            pltpu.VMEM((tm + HALO, d), jnp.bfloat16),
            pltpu.VMEM((tm + 2 * HALO, tf), jnp.float32),
            pltpu.VMEM((tm + 2 * HALO, tf), jnp.float32),
            pltpu.VMEM((tm, d), jnp.float32),
            pltpu.SemaphoreType.DMA(()),
        ],
        compiler_params=_params(2),
        name="conv_ffn",
    )(x, x, x, g_all, w_up, w_up, cp_all, w_down, final_g, *[c[0] for c in casts])
    return outs[0], outs[1:]


def kernel(x, mem, mix_norm_g, ffn_norm_g, mem_norm_g, w_mem_kv, a_w_in, a_w_out, b_w_in,
           b_v_norm_g, b_w_s, b_s_bias, b_w_out, ffn_w_up, ffn_conv_w, ffn_conv_b, ffn_w_down,
           final_norm_g):
    batch, seq, d = x.shape
    assert (seq, d) == (SEQ, D_MODEL)
    bf = jnp.bfloat16
    xs = x.reshape(batch * seq, d)
    mems = mem.reshape(batch * MEM_LEN, d)

    mix_g = mix_norm_g.reshape(DEPTH, 1, d)
    ffn_g = ffn_norm_g.reshape(DEPTH, 1, d)
    mem_g = mem_norm_g.reshape(DEPTH, 1, d)
    final_g = final_norm_g.reshape(1, d)
    v_norm_g = b_v_norm_g.reshape(-1, 1, B_W)
    s_bias_t = jnp.swapaxes(b_s_bias, 1, 2)
    conv_p = jnp.concatenate([ffn_conv_w.reshape(DEPTH, CONV_TAPS, 2, FF).transpose(0, 2, 1, 3),
                              ffn_conv_b.reshape(DEPTH, 2, 1, FF)], axis=2)
    b_w_s = b_w_s.astype(bf)
    kv = _mem_kv(mems, mem_g, w_mem_kv).reshape(DEPTH * batch, MEM_LEN, 2 * MEM_W)

    def mixer_weights(i):
        return (a_w_in, a_w_out) if i % 2 == 0 else (b_w_in, b_w_out)

    w_in = a_w_in[0].astype(bf)
    w_out = w_up = w_down = None

    for i in range(DEPTH):
        j = i // 2
        if i % 2 == 0:
            first = i == 0
            qkv, qm, cast = _in_proj_a(xs, mix_g, i, w_in,
                                       [(a_w_out, 0), (ffn_w_down, 0)] if first else [],
                                       tm=512, tn=1280)
            if first:
                w_out, w_down = cast
            tok, cast = _dilated_attention(qkv, [(ffn_w_up, 0)] if first else [], batch=batch)
            if first:
                w_up, = cast
            xs = _mix_out_a(xs, tok, qm, kv, w_out, i, tm=1024)
        else:
            xs = _mixer_b(xs, mix_g, w_in, kv, v_norm_g, b_w_s, s_bias_t, w_out, i, j,
                          tm=512, tn=1792)
        casts = []
        if i + 1 < DEPTH:
            nxt_in, nxt_out = mixer_weights(i + 1)
            casts = [(nxt_in, (i + 1) // 2), (nxt_out, (i + 1) // 2), (ffn_w_up, i + 1),
                     (ffn_w_down, i + 1)]
        xs, nxt = _conv_ffn(xs, ffn_g, w_up, w_down, conv_p, final_g, i, casts,
                            tm=FFN_TM, tf=FFN_TF, final_norm=(i == DEPTH - 1))
        if nxt:
            w_in, w_out, w_up, w_down = nxt
    return xs.reshape(batch, seq, d)
```

```python
import functools

import numpy as np
import jax
import jax.numpy as jnp
from jax import lax
from jax.experimental import pallas as pl
from jax.experimental.pallas import tpu as pltpu

D_MODEL = 2048
SEQ = 2048
DEPTH = 4
EPS = 1e-6
NEG = -1e30

HEAD_DIM = 128
HEADS_PER_GROUP = 4
A_PATTERNS = ((128, 1), (512, 4), (2048, 16))
A_GROUPS = len(A_PATTERNS)
A_HEADS = HEADS_PER_GROUP * A_GROUPS
A_QKV_W = A_HEADS * HEAD_DIM
GROUP_W = HEADS_PER_GROUP * HEAD_DIM
QBLK = 128

CHUNK = 128
B_GROUPS = 12
B_GROUP_W = 128
B_W = B_GROUPS * B_GROUP_W

MEM_LEN = 256
MEM_HEADS = 4
MEM_W = MEM_HEADS * HEAD_DIM

A_IN = 3 * A_QKV_W + MEM_W
B_IN = 2 * B_W + MEM_W
FF = 5632

VMEM_LIMIT_BYTES = 62 * 1024 * 1024

_SQRT_HALF = 0.7071067811865476


def _params(n_axes):
    return pltpu.CompilerParams(
        dimension_semantics=("arbitrary",) * n_axes,
        vmem_limit_bytes=VMEM_LIMIT_BYTES,
    )


def _rms(x, g):
    y = x * lax.rsqrt(jnp.mean(x * x, axis=-1, keepdims=True) + EPS)
    return y * g


NORM_ROWS = 16


def _rms_to(dst_ref, x_ref, g, copy_ref=None):
    n = x_ref.shape[0]
    for r0 in range(0, n, NORM_ROWS):
        rows = slice(r0, r0 + NORM_ROWS)
        x = x_ref[rows, :]
        dst_ref[rows, :] = _rms(x, g).astype(dst_ref.dtype)
        if copy_ref is not None:
            copy_ref[rows, :] = x


def _gelu(x):
    return 0.5 * x * (1.0 + lax.erf(x * _SQRT_HALF))


def _alibi_slopes():
    return (2.0 ** (-8.0 * (np.arange(A_HEADS) + 1) / A_HEADS)).astype(np.float32)


def _layer_spec(block, index_map, layer):
    return pl.BlockSpec((None,) + block, lambda *g: (layer,) + index_map(*g))


def _resident(shape):
    return pl.BlockSpec(shape, lambda *g: (0,) * len(shape), pipeline_mode=pl.Buffered(1))


BF16_ROWS = 16


def _cast_specs(casts, n_steps, step_of):
    in_specs, out_specs, out_shapes = [], [], []
    for src, src_layer in casts:
        _, rows, width = src.shape
        block_rows = BF16_ROWS * pl.cdiv(rows, BF16_ROWS * n_steps)
        n_blocks = rows // block_rows
        assert rows % block_rows == 0 and n_blocks <= n_steps

        def imap(*g, n_blocks=n_blocks):
            return (jnp.minimum(step_of(*g), n_blocks - 1), 0)

        in_specs.append(_layer_spec((block_rows, width), imap, src_layer))
        out_specs.append(pl.BlockSpec((block_rows, width), imap))
        out_shapes.append(jax.ShapeDtypeStruct((rows, width), jnp.bfloat16))
    return in_specs, out_specs, out_shapes


def _run_casts(cast_in, cast_out):
    for src, dst in zip(cast_in, cast_out):
        dst[...] = src[...].astype(jnp.bfloat16)


N_SLABS = 4
GATHER_STRIDE = 4


def _in_proj_a_kernel(x_ref, g_ref, w_ref, *refs, tn, n_casts):
    cast_in, refs = refs[:n_casts], refs[n_casts:]
    group_refs = refs[:A_GROUPS]
    qm_ref = refs[A_GROUPS]
    cast_out = refs[A_GROUPS + 1:A_GROUPS + 1 + n_casts]
    h_ref, slab_ref, part_ref = refs[A_GROUPS + 1 + n_casts:]
    tm = x_ref.shape[0]
    _run_casts(cast_in, cast_out)
    _rms_to(h_ref, x_ref, g_ref[...])
    n_slab = 0
    for c in reversed(range(w_ref.shape[1] // tn)):
        acc = jnp.dot(h_ref[...], w_ref[:, c * tn:(c + 1) * tn],
                      preferred_element_type=jnp.float32)
        for cc in range(tn // HEAD_DIM):
            col = c * (tn // HEAD_DIM) + cc
            slab = acc[:, cc * HEAD_DIM:(cc + 1) * HEAD_DIM]
            which, rest = divmod(col, A_HEADS)
            if which == 3:
                qm_ref[rest] = slab.astype(qm_ref.dtype)
                continue
            grp, head = divmod(rest, HEADS_PER_GROUP)
            dst = group_refs[grp]
            dilation = A_PATTERNS[grp][1]
            if dilation == 1:
                dst[which * HEADS_PER_GROUP + head, 0] = slab.astype(dst.dtype)
                continue
            buf = n_slab % N_SLABS
            n_slab += 1
            slab_ref[buf] = slab
            out_col = which * HEADS_PER_GROUP + head
            if dilation <= GATHER_STRIDE:
                for r in range(dilation):
                    rows = slab_ref[buf, pl.ds(r, tm // dilation, stride=dilation), :]
                    dst[out_col, r] = rows.astype(dst.dtype)
                continue
            outer = dilation // GATHER_STRIDE
            for r1 in range(GATHER_STRIDE):
                part_ref[buf, r1] = slab_ref[buf, pl.ds(r1, tm // GATHER_STRIDE,
                                                        stride=GATHER_STRIDE), :]
            for r1 in range(GATHER_STRIDE):
                for k in range(outer):
                    rows = part_ref[buf, r1, pl.ds(k, tm // dilation, stride=outer), :]
                    dst[out_col, r1 + GATHER_STRIDE * k] = rows.astype(dst.dtype)


def _in_proj_a(x, g_all, layer_g, w, casts, *, tm, tn):
    m, d = x.shape
    n = w.shape[1]
    batch = m // SEQ
    tiles_per_seq = SEQ // tm
    assert m % tm == 0 and SEQ % tm == 0 and n == A_IN and n % tn == 0 and tn % HEAD_DIM == 0
    out_specs, out_shapes = [], []
    for _, dilation in A_PATTERNS:
        assert tm % (dilation * BF16_ROWS) == 0
        out_specs.append(pl.BlockSpec(
            (3 * HEADS_PER_GROUP, None, dilation, tm // dilation, HEAD_DIM),
            lambda i: (0, i // tiles_per_seq, 0, i % tiles_per_seq, 0)))
        out_shapes.append(jax.ShapeDtypeStruct(
            (3 * HEADS_PER_GROUP, batch, dilation, SEQ // dilation, HEAD_DIM), jnp.bfloat16))
    out_specs.append(pl.BlockSpec((MEM_HEADS, tm, HEAD_DIM), lambda i: (0, i, 0)))
    out_shapes.append(jax.ShapeDtypeStruct((MEM_HEADS, m, HEAD_DIM), jnp.bfloat16))
    cast_in_specs, cast_out_specs, cast_out_shapes = _cast_specs(casts, m // tm, lambda i: i)
    outs = pl.pallas_call(
        functools.partial(_in_proj_a_kernel, tn=tn, n_casts=len(casts)),
        grid=(m // tm,),
        in_specs=[
            pl.BlockSpec((tm, d), lambda i: (i, 0)),
            _layer_spec((1, d), lambda i: (0, 0), layer_g),
            _resident(w.shape),
        ] + cast_in_specs,
        out_specs=out_specs + cast_out_specs,
        out_shape=out_shapes + cast_out_shapes,
        scratch_shapes=[pltpu.VMEM((tm, d), jnp.bfloat16),
                        pltpu.VMEM((N_SLABS, tm, HEAD_DIM), jnp.float32),
                        pltpu.VMEM((N_SLABS, GATHER_STRIDE, tm // GATHER_STRIDE, HEAD_DIM),
                                   jnp.float32)],
        compiler_params=_params(1),
        name="in_proj_a",
    )(x, g_all, w, *[c[0] for c in casts])
    return outs[:A_GROUPS], outs[A_GROUPS], outs[A_GROUPS + 1:]


def _mem_kv_kernel(x_ref, g_ref, w_ref, o_ref, h_ref):
    _rms_to(h_ref, x_ref, g_ref[...])
    w = w_ref[...].astype(jnp.bfloat16)
    o_ref[...] = jnp.dot(h_ref[...], w, preferred_element_type=jnp.float32).astype(o_ref.dtype)


def _mem_kv(mems, g_all, w_all):
    m, d = mems.shape
    layers, _, n = w_all.shape
    return pl.pallas_call(
        _mem_kv_kernel,
        grid=(layers,),
        in_specs=[
            pl.BlockSpec((m, d), lambda l: (0, 0)),
            pl.BlockSpec((None, 1, d), lambda l: (l, 0, 0)),
            pl.BlockSpec((None, d, n), lambda l: (l, 0, 0)),
        ],
        out_specs=pl.BlockSpec((None, m, n), lambda l: (l, 0, 0)),
        out_shape=jax.ShapeDtypeStruct((layers, m, n), jnp.bfloat16),
        scratch_shapes=[pltpu.VMEM((m, d), jnp.bfloat16)],
        compiler_params=_params(1),
        name="mem_kv",
    )(mems, g_all, w_all)


def _group_geometry(grp):
    window, dilation = A_PATTERNS[grp]
    n_side = (window // 2) // dilation
    length = SEQ // dilation
    kw = min(length, QBLK + 2 * n_side)
    return dilation, n_side, length, kw


def _key_start(qi, n_side, length, kw):
    return min(max(qi * QBLK - n_side, 0), length - kw)


def _dilated_attn_kernel(slopes_ref, *refs, n_casts):
    qkv_refs, refs = refs[:3 * A_GROUPS], refs[3 * A_GROUPS:]
    cast_in, refs = refs[:n_casts], refs[n_casts:]
    tok_ref, refs = refs[0], refs[1:]
    cast_out, refs = refs[:n_casts], refs[n_casts:]
    o_scr, lse_scr, bias_scr = refs[:3]
    stage = refs[3:]
    head = pl.program_id(1)
    scale = HEAD_DIM ** -0.5
    _run_casts(cast_in, cast_out)

    for grp in range(A_GROUPS):
        dilation, n_side, length, kw = _group_geometry(grp)
        nblk = length // QBLK
        q_ref, k_ref, v_ref = qkv_refs[3 * grp:3 * grp + 3]
        vs_ref = stage[grp]
        slope = slopes_ref[grp, head]

        vs_ref[:, :, 0:HEAD_DIM] = v_ref[...]
        vs_ref[:, :, HEAD_DIM:2 * HEAD_DIM] = jnp.ones((dilation, length, HEAD_DIM), jnp.bfloat16)

        offsets = sorted({qi * QBLK - _key_start(qi, n_side, length, kw) for qi in range(nblk)})
        for t, off in enumerate(offsets):
            rel = (lax.broadcasted_iota(jnp.int32, (QBLK, kw), 1)
                   - lax.broadcasted_iota(jnp.int32, (QBLK, kw), 0) - off)
            dist = jnp.abs(rel)
            alibi = (-slope) * (dist * dilation).astype(jnp.float32)
            bias_scr[grp, t, :, 0:kw] = jnp.where(dist <= n_side, alibi, NEG)

        for r in range(dilation):
            for qi in range(nblk):
                k0 = _key_start(qi, n_side, length, kw)
                t = offsets.index(qi * QBLK - k0)
                q = q_ref[r, qi * QBLK:(qi + 1) * QBLK, :]
                k = k_ref[r, k0:k0 + kw, :]
                v1 = vs_ref[r, k0:k0 + kw, :]
                bias = bias_scr[grp, t, :, 0:kw]
                s = lax.dot_general(q, k, (((1,), (1,)), ((), ())),
                                    preferred_element_type=jnp.float32) * scale
                s = jnp.where(bias > 0.5 * NEG, s + bias, NEG)
                m = jnp.max(s, axis=-1, keepdims=True)
                p = jnp.exp(s - m).astype(jnp.bfloat16)
                ol = jnp.dot(p, v1, preferred_element_type=jnp.float32)
                l = ol[:, HEAD_DIM:]
                start = qi * QBLK * dilation + r
                dst = pl.ds(start, QBLK, stride=dilation) if dilation > 1 else pl.ds(start, QBLK)
                o_scr[grp, dst, :] = ol[:, :HEAD_DIM] / l
                lse_scr[grp, dst, :] = m + jnp.log(l)

    rows_per_step = 256
    for c in range(SEQ // rows_per_step):
        rows = slice(c * rows_per_step, (c + 1) * rows_per_step)
        lses = [lse_scr[grp, rows, :] for grp in range(A_GROUPS)]
        mx = functools.reduce(jnp.maximum, lses)
        es = [jnp.exp(l - mx) for l in lses]
        den = functools.reduce(lambda a, b: a + b, es)
        tok = sum((e / den) * o_scr[grp, rows, :] for grp, e in enumerate(es))
        tok_ref[rows, :] = tok.astype(tok_ref.dtype)


def _dilated_attention(qkv, casts, *, batch):
    slopes = jnp.asarray(_alibi_slopes().reshape(A_GROUPS, HEADS_PER_GROUP))
    in_specs, operands, stage = [], [], []
    for grp in range(A_GROUPS):
        dilation, _, length, _ = _group_geometry(grp)
        for which in range(3):
            in_specs.append(pl.BlockSpec(
                (None, None, dilation, length, HEAD_DIM),
                lambda b, h, sl, which=which: (which * HEADS_PER_GROUP + h, b, 0, 0, 0)))
            operands.append(qkv[grp])
        stage.append(pltpu.VMEM((dilation, length, 2 * HEAD_DIM), jnp.bfloat16))
    max_kw = max(_group_geometry(grp)[3] for grp in range(A_GROUPS))
    cast_in_specs, cast_out_specs, cast_out_shapes = _cast_specs(
        casts, batch * HEADS_PER_GROUP, lambda b, h, sl: b * HEADS_PER_GROUP + h)
    outs = pl.pallas_call(
        functools.partial(_dilated_attn_kernel, n_casts=len(casts)),
        grid_spec=pltpu.PrefetchScalarGridSpec(
            num_scalar_prefetch=1,
            grid=(batch, HEADS_PER_GROUP),
            in_specs=in_specs + cast_in_specs,
            out_specs=[pl.BlockSpec((SEQ, HEAD_DIM), lambda b, h, sl: (b, h))] + cast_out_specs,
            scratch_shapes=[
                pltpu.VMEM((A_GROUPS, SEQ, HEAD_DIM), jnp.float32),
                pltpu.VMEM((A_GROUPS, SEQ, HEAD_DIM), jnp.float32),
                pltpu.VMEM((A_GROUPS, 3, QBLK, max_kw), jnp.float32),
            ] + stage,
        ),
        out_shape=[jax.ShapeDtypeStruct((batch * SEQ, GROUP_W), jnp.bfloat16)] + cast_out_shapes,
        compiler_params=_params(2),
        name="dilated_attn",
    )(slopes, *operands, *[c[0] for c in casts])
    return outs[0], outs[1:]


def _mem_attention(q_heads, kv_ref):
    scale = HEAD_DIM ** -0.5
    outs = []
    for h, q in enumerate(q_heads):
        k = kv_ref[:, h * HEAD_DIM:(h + 1) * HEAD_DIM]
        v = kv_ref[:, MEM_W + h * HEAD_DIM:MEM_W + (h + 1) * HEAD_DIM]
        s = lax.dot_general(q, k, (((1,), (1,)), ((), ())),
                            preferred_element_type=jnp.float32) * scale
        m = jnp.max(s, axis=-1, keepdims=True)
        p = jnp.exp(s - m)
        l = jnp.sum(p, axis=-1, keepdims=True)
        o = jnp.dot(p.astype(jnp.bfloat16), v, preferred_element_type=jnp.float32)
        outs.append(o / l)
    return jnp.concatenate(outs, axis=-1)


def _mix_out_a_kernel(x_ref, tok_ref, qm_ref, kv_ref, w_ref, out_ref):
    q_heads = [qm_ref[h] for h in range(MEM_HEADS)]
    mem_out = _mem_attention(q_heads, kv_ref).astype(jnp.bfloat16)
    cat = jnp.concatenate([tok_ref[...], mem_out], axis=-1)
    out_ref[...] = x_ref[...] + jnp.dot(cat, w_ref[...], preferred_element_type=jnp.float32)


def _kv_spec(layer, batch, tiles_per_seq):
    return pl.BlockSpec((None, MEM_LEN, 2 * MEM_W),
                        lambda i: (layer * batch + i // tiles_per_seq, 0, 0))


def _mix_out_a(x, tok, qm, kv, w_out, layer, *, tm):
    m, d = x.shape
    tiles_per_seq = SEQ // tm
    batch = m // SEQ
    row = lambda i: (i, 0)
    return pl.pallas_call(
        _mix_out_a_kernel,
        grid=(m // tm,),
        in_specs=[
            pl.BlockSpec((tm, d), row),
            pl.BlockSpec((tm, GROUP_W), row),
            pl.BlockSpec((MEM_HEADS, tm, HEAD_DIM), lambda i: (0, i, 0)),
            _kv_spec(layer, batch, tiles_per_seq),
            pl.BlockSpec(w_out.shape, lambda i: (0, 0)),
        ],
        out_specs=pl.BlockSpec((tm, d), row),
        out_shape=jax.ShapeDtypeStruct((m, d), jnp.float32),
        compiler_params=_params(1),
        name="mix_out_a",
    )(x, tok, qm, kv, w_out)


def _mixer_b_kernel(x_ref, g_ref, win_ref, vg_ref, ws_ref, sb_ref, kv_ref, wout_ref, out_ref,
                    h_ref, uvq_ref, tok_ref, *, tn):
    tm = x_ref.shape[0]
    _rms_to(h_ref, x_ref, g_ref[...])
    for c in range(B_IN // tn):
        acc = jnp.dot(h_ref[...], win_ref[:, c * tn:(c + 1) * tn],
                      preferred_element_type=jnp.float32)
        n_gelu = min(max(2 * B_W - c * tn, 0), tn)
        if n_gelu:
            uvq_ref[:, c * tn:c * tn + n_gelu] = _gelu(acc[:, :n_gelu])
        if n_gelu < tn:
            uvq_ref[:, c * tn + n_gelu:(c + 1) * tn] = acc[:, n_gelu:]

    vn = _rms(uvq_ref[:, B_W:2 * B_W], vg_ref[...]).astype(jnp.bfloat16)
    n_chunks = tm // CHUNK
    for g in range(B_GROUPS):
        cols = slice(g * B_GROUP_W, (g + 1) * B_GROUP_W)
        v_g = jnp.concatenate([vn[c * CHUNK:(c + 1) * CHUNK, cols] for c in range(n_chunks)],
                              axis=1)
        mixed = jnp.dot(ws_ref[g], v_g, preferred_element_type=jnp.float32)
        mixed = mixed + sb_ref[:, g:g + 1]
        for c in range(n_chunks):
            rows = slice(c * CHUNK, (c + 1) * CHUNK)
            tok_ref[rows, cols] = (uvq_ref[rows, cols]
                                   * mixed[:, c * B_GROUP_W:(c + 1) * B_GROUP_W]
                                   ).astype(jnp.bfloat16)
    qm = uvq_ref[:, 2 * B_W:].astype(jnp.bfloat16)
    q_heads = [qm[:, hd * HEAD_DIM:(hd + 1) * HEAD_DIM] for hd in range(MEM_HEADS)]
    mem_out = _mem_attention(q_heads, kv_ref).astype(jnp.bfloat16)
    cat = jnp.concatenate([tok_ref[...], mem_out], axis=-1)
    out_ref[...] = x_ref[...] + jnp.dot(cat, wout_ref[...], preferred_element_type=jnp.float32)


def _mixer_b(x, g_all, w_in, kv, v_norm_g_all, w_s_all, s_bias_t_all, w_out, layer, layer_b, *,
             tm, tn):
    m, d = x.shape
    assert m % tm == 0 and SEQ % tm == 0 and tm % CHUNK == 0 and B_IN % tn == 0
    tiles_per_seq = SEQ // tm
    batch = m // SEQ
    row = lambda i: (i, 0)
    const2 = lambda i: (0, 0)
    return pl.pallas_call(
        functools.partial(_mixer_b_kernel, tn=tn),
        grid=(m // tm,),
        in_specs=[
            pl.BlockSpec((tm, d), row),
            _layer_spec((1, d), const2, layer),
            _resident(w_in.shape),
            _layer_spec((1, B_W), const2, layer_b),
            _layer_spec(w_s_all.shape[1:], lambda i: (0, 0, 0), layer_b),
            _layer_spec(s_bias_t_all.shape[1:], const2, layer_b),
            _kv_spec(layer, batch, tiles_per_seq),
            _resident(w_out.shape),
        ],
        out_specs=pl.BlockSpec((tm, d), row),
        out_shape=jax.ShapeDtypeStruct((m, d), jnp.float32),
        scratch_shapes=[
            pltpu.VMEM((tm, d), jnp.bfloat16),
            pltpu.VMEM((tm, B_IN), jnp.float32),
            pltpu.VMEM((tm, B_W), jnp.bfloat16),
        ],
        compiler_params=_params(1),
        name="mixer_b",
    )(x, g_all, w_in, v_norm_g_all, w_s_all, s_bias_t_all, kv, w_out)


HALO = BF16_ROWS
FFN_TM = 1024
FFN_TF = 512
MXU_N = 256


N_FFN_IN = 9
CONV_TAPS = 3


def _conv_ffn_kernel(*refs, n_casts, tiles_per_seq, final_norm):
    (x_hbm, xp_ref, xn_ref, g_ref, wg_ref, wv_ref, cp_ref, wd_ref,
     fg_ref) = refs[:N_FFN_IN]
    cast_in = refs[N_FFN_IN:N_FFN_IN + n_casts]
    out_ref = refs[N_FFN_IN + n_casts]
    cast_out = refs[N_FFN_IN + n_casts + 1:N_FFN_IN + 2 * n_casts + 1]
    h_ref, ag_ref, av_ref, x_ref, x_sem = refs[N_FFN_IN + 2 * n_casts + 1:]
    i = pl.program_id(0)
    j = pl.program_id(1)
    n_tiles = pl.num_programs(0)
    nj = pl.num_programs(1)
    tm = x_ref.shape[0]

    def x_copy(tile):
        return pltpu.make_async_copy(x_hbm.at[pl.ds(tile * tm, tm), :], x_ref, x_sem)

    @pl.when(jnp.logical_and(i == 0, j == 0))
    def _():
        x_copy(0).start()

    @pl.when(j == 0)
    def _():
        x_copy(i).wait()
        g = g_ref[...]
        _rms_to(h_ref, x_ref, g, copy_ref=out_ref)
        first = (i % tiles_per_seq) == 0
        last = (i % tiles_per_seq) == tiles_per_seq - 1
        r = lax.broadcasted_iota(jnp.int32, (HALO, 1), 0)
        take_next = jnp.logical_and(r == 0, jnp.logical_not(last))
        take_prev = jnp.logical_and(r == HALO - 1, jnp.logical_not(first))
        halo = jnp.where(take_next, _rms(xn_ref[...], g),
                         jnp.where(take_prev, _rms(xp_ref[...], g), 0.0))
        h_ref[tm:tm + HALO, :] = halo.astype(jnp.bfloat16)

    @pl.when(jnp.logical_and(j == 1, i + 1 < n_tiles))
    def _():
        x_copy(i + 1).start()

    _run_casts(cast_in, cast_out)

    def up(w_ref, a_ref):
        a = jnp.dot(h_ref[...], w_ref[...], preferred_element_type=jnp.float32)
        a_ref[HALO:HALO + tm, :] = a[0:tm]
        a_ref[0:HALO, :] = a[tm:tm + HALO]
        a_ref[HALO + tm:2 * HALO + tm, :] = a[tm:tm + HALO]

    def conv(a_ref, half, cols):
        cp = cp_ref[half]
        return (a_ref[HALO - 1:HALO - 1 + tm, cols] * cp[0:1, cols]
                + a_ref[HALO:HALO + tm, cols] * cp[1:2, cols]
                + a_ref[HALO + 1:HALO + 1 + tm, cols] * cp[2:3, cols]
                + cp[CONV_TAPS:CONV_TAPS + 1, cols])

    up(wg_ref, ag_ref)
    up(wv_ref, av_ref)
    tf = wg_ref.shape[1]
    upd = None
    for c in range(tf // MXU_N):
        cols = slice(c * MXU_N, (c + 1) * MXU_N)
        act = (_gelu(conv(ag_ref, 0, cols)) * conv(av_ref, 1, cols)).astype(jnp.bfloat16)
        part = jnp.dot(act, wd_ref[cols, :], preferred_element_type=jnp.float32)
        upd = part if upd is None else upd + part
    out_ref[...] += upd

    if final_norm:
        @pl.when(j == nj - 1)
        def _():
            _rms_to(out_ref, out_ref, fg_ref[...])


def _conv_ffn(x, g_all, w_up, w_down, cp_all, final_g, layer, casts, *, tm, tf, final_norm):
    m, d = x.shape
    nf = FF // tf
    assert m % tm == 0 and FF % tf == 0 and SEQ % tm == 0 and tm % HALO == 0
    assert nf >= 2
    n_tiles = m // tm
    n_steps = n_tiles * nf
    tiles_per_seq = SEQ // tm
    hb = tm // HALO
    n_hblocks = m // HALO

    cast_in_specs, cast_out_specs, cast_out_shapes = _cast_specs(
        casts, n_steps, lambda i, j: i * nf + j)

    outs = pl.pallas_call(
        functools.partial(_conv_ffn_kernel, n_casts=len(casts),
                          tiles_per_seq=tiles_per_seq, final_norm=final_norm),
        grid=(n_tiles, nf),
        in_specs=[
            pl.BlockSpec(memory_space=pl.ANY),
            pl.BlockSpec((HALO, d), lambda i, j: (jnp.maximum(i * hb - 1, 0), 0)),
            pl.BlockSpec((HALO, d), lambda i, j: (jnp.minimum((i + 1) * hb, n_hblocks - 1), 0)),
            _layer_spec((1, d), lambda i, j: (0, 0), layer),
            pl.BlockSpec((d, tf), lambda i, j: (0, j)),
            pl.BlockSpec((d, tf), lambda i, j: (0, nf + j)),
            _layer_spec((2, CONV_TAPS + 1, tf), lambda i, j: (0, 0, j), layer),
            pl.BlockSpec((tf, d), lambda i, j: (j, 0)),
            pl.BlockSpec((1, d), lambda i, j: (0, 0)),
        ] + cast_in_specs,
        out_specs=[pl.BlockSpec((tm, d), lambda i, j: (i, 0))] + cast_out_specs,
        out_shape=[jax.ShapeDtypeStruct((m, d), jnp.float32)] + cast_out_shapes,
        scratch_shapes=[
            pltpu.VMEM((tm + HALO, d), jnp.bfloat16),
            pltpu.VMEM((tm + 2 * HALO, tf), jnp.float32),
            pltpu.VMEM((tm + 2 * HALO, tf), jnp.float32),
            pltpu.VMEM((tm, d), jnp.float32),
            pltpu.SemaphoreType.DMA(()),
        ],
        compiler_params=_params(2),
        name="conv_ffn",
    )(x, x, x, g_all, w_up, w_up, cp_all, w_down, final_g, *[c[0] for c in casts])
    return outs[0], outs[1:]


def kernel(x, mem, mix_norm_g, ffn_norm_g, mem_norm_g, w_mem_kv, a_w_in, a_w_out, b_w_in,
           b_v_norm_g, b_w_s, b_s_bias, b_w_out, ffn_w_up, ffn_conv_w, ffn_conv_b, ffn_w_down,
           final_norm_g):
    batch, seq, d = x.shape
    assert (seq, d) == (SEQ, D_MODEL)
    bf = jnp.bfloat16
    xs = x.reshape(batch * seq, d)
    mems = mem.reshape(batch * MEM_LEN, d)

    mix_g = mix_norm_g.reshape(DEPTH, 1, d)
    ffn_g = ffn_norm_g.reshape(DEPTH, 1, d)
    mem_g = mem_norm_g.reshape(DEPTH, 1, d)
    final_g = final_norm_g.reshape(1, d)
    v_norm_g = b_v_norm_g.reshape(-1, 1, B_W)
    s_bias_t = jnp.swapaxes(b_s_bias, 1, 2)
    conv_p = jnp.concatenate([ffn_conv_w.reshape(DEPTH, CONV_TAPS, 2, FF).transpose(0, 2, 1, 3),
                              ffn_conv_b.reshape(DEPTH, 2, 1, FF)], axis=2)
    b_w_s = b_w_s.astype(bf)
    kv = _mem_kv(mems, mem_g, w_mem_kv).reshape(DEPTH * batch, MEM_LEN, 2 * MEM_W)

    def mixer_weights(i):
        return (a_w_in, a_w_out) if i % 2 == 0 else (b_w_in, b_w_out)

    w_in = a_w_in[0].astype(bf)
    w_out = w_up = w_down = None

    for i in range(DEPTH):
        j = i // 2
        if i % 2 == 0:
            first = i == 0
            qkv, qm, cast = _in_proj_a(xs, mix_g, i, w_in,
                                       [(a_w_out, 0), (ffn_w_down, 0)] if first else [],
                                       tm=512, tn=1280)
            if first:
                w_out, w_down = cast
            tok, cast = _dilated_attention(qkv, [(ffn_w_up, 0)] if first else [], batch=batch)
            if first:
                w_up, = cast
            xs = _mix_out_a(xs, tok, qm, kv, w_out, i, tm=1024)
        else:
            xs = _mixer_b(xs, mix_g, w_in, kv, v_norm_g, b_w_s, s_bias_t, w_out, i, j,
                          tm=512, tn=1792)
        casts = []
        if i + 1 < DEPTH:
            nxt_in, nxt_out = mixer_weights(i + 1)
            casts = [(nxt_in, (i + 1) // 2), (nxt_out, (i + 1) // 2), (ffn_w_up, i + 1),
                     (ffn_w_down, i + 1)]
        xs, nxt = _conv_ffn(xs, ffn_g, w_up, w_down, conv_p, final_g, i, casts,
                            tm=FFN_TM, tf=FFN_TF, final_norm=(i == DEPTH - 1))
        if nxt:
            w_in, w_out, w_up, w_down = nxt
    return xs.reshape(batch, seq, d)
```

```python
import functools

import numpy as np
import jax
import jax.numpy as jnp
from jax import lax
from jax.experimental import pallas as pl
from jax.experimental.pallas import tpu as pltpu

D_MODEL = 2048
SEQ = 2048
DEPTH = 4
EPS = 1e-6
NEG = -1e30

HEAD_DIM = 128
HEADS_PER_GROUP = 4
A_PATTERNS = ((128, 1), (512, 4), (2048, 16))
A_GROUPS = len(A_PATTERNS)
A_HEADS = HEADS_PER_GROUP * A_GROUPS
A_QKV_W = A_HEADS * HEAD_DIM
GROUP_W = HEADS_PER_GROUP * HEAD_DIM
QBLK = 128

CHUNK = 128
B_GROUPS = 12
B_GROUP_W = 128
B_W = B_GROUPS * B_GROUP_W

MEM_LEN = 256
MEM_HEADS = 4
MEM_W = MEM_HEADS * HEAD_DIM

A_IN = 3 * A_QKV_W + MEM_W
B_IN = 2 * B_W + MEM_W
FF = 5632

VMEM_LIMIT_BYTES = 62 * 1024 * 1024

_SQRT_HALF = 0.7071067811865476


def _params(n_axes):
    return pltpu.CompilerParams(
        dimension_semantics=("arbitrary",) * n_axes,
        vmem_limit_bytes=VMEM_LIMIT_BYTES,
    )


def _rms(x, g):
    y = x * lax.rsqrt(jnp.mean(x * x, axis=-1, keepdims=True) + EPS)
    return y * g


NORM_ROWS = 16


def _rms_to(dst_ref, x_ref, g, copy_ref=None):
    n = x_ref.shape[0]
    for r0 in range(0, n, NORM_ROWS):
        rows = slice(r0, r0 + NORM_ROWS)
        x = x_ref[rows, :]
        dst_ref[rows, :] = _rms(x, g).astype(dst_ref.dtype)
        if copy_ref is not None:
            copy_ref[rows, :] = x


def _gelu(x):
    return 0.5 * x * (1.0 + lax.erf(x * _SQRT_HALF))


def _alibi_slopes():
    return (2.0 ** (-8.0 * (np.arange(A_HEADS) + 1) / A_HEADS)).astype(np.float32)


def _layer_spec(block, index_map, layer):
    return pl.BlockSpec((None,) + block, lambda *g: (layer,) + index_map(*g))


def _resident(shape):
    return pl.BlockSpec(shape, lambda *g: (0,) * len(shape), pipeline_mode=pl.Buffered(1))


BF16_ROWS = 16


def _cast_specs(casts, n_steps, step_of):
    in_specs, out_specs, out_shapes = [], [], []
    for src, src_layer in casts:
        _, rows, width = src.shape
        block_rows = BF16_ROWS * pl.cdiv(rows, BF16_ROWS * n_steps)
        n_blocks = rows // block_rows
        assert rows % block_rows == 0 and n_blocks <= n_steps

        def imap(*g, n_blocks=n_blocks):
            return (jnp.minimum(step_of(*g), n_blocks - 1), 0)

        in_specs.append(_layer_spec((block_rows, width), imap, src_layer))
        out_specs.append(pl.BlockSpec((block_rows, width), imap))
        out_shapes.append(jax.ShapeDtypeStruct((rows, width), jnp.bfloat16))
    return in_specs, out_specs, out_shapes


def _run_casts(cast_in, cast_out):
    for src, dst in zip(cast_in, cast_out):
        dst[...] = src[...].astype(jnp.bfloat16)


N_SLABS = 4
GATHER_STRIDE = 4


def _in_proj_a_kernel(x_ref, g_ref, w_ref, *refs, tn, n_casts):
    cast_in, refs = refs[:n_casts], refs[n_casts:]
    group_refs = refs[:A_GROUPS]
    qm_ref = refs[A_GROUPS]
    cast_out = refs[A_GROUPS + 1:A_GROUPS + 1 + n_casts]
    h_ref, slab_ref, part_ref = refs[A_GROUPS + 1 + n_casts:]
    tm = x_ref.shape[0]
    _run_casts(cast_in, cast_out)
    _rms_to(h_ref, x_ref, g_ref[...])
    n_slab = 0
    for c in reversed(range(w_ref.shape[1] // tn)):
        acc = jnp.dot(h_ref[...], w_ref[:, c * tn:(c + 1) * tn],
                      preferred_element_type=jnp.float32)
        for cc in range(tn // HEAD_DIM):
            col = c * (tn // HEAD_DIM) + cc
            slab = acc[:, cc * HEAD_DIM:(cc + 1) * HEAD_DIM]
            which, rest = divmod(col, A_HEADS)
            if which == 3:
                qm_ref[rest] = slab.astype(qm_ref.dtype)
                continue
            grp, head = divmod(rest, HEADS_PER_GROUP)
            dst = group_refs[grp]
            dilation = A_PATTERNS[grp][1]
            if dilation == 1:
                dst[which * HEADS_PER_GROUP + head, 0] = slab.astype(dst.dtype)
                continue
            buf = n_slab % N_SLABS
            n_slab += 1
            slab_ref[buf] = slab
            out_col = which * HEADS_PER_GROUP + head
            if dilation <= GATHER_STRIDE:
                for r in range(dilation):
                    rows = slab_ref[buf, pl.ds(r, tm // dilation, stride=dilation), :]
                    dst[out_col, r] = rows.astype(dst.dtype)
                continue
            outer = dilation // GATHER_STRIDE
            for r1 in range(GATHER_STRIDE):
                part_ref[buf, r1] = slab_ref[buf, pl.ds(r1, tm // GATHER_STRIDE,
                                                        stride=GATHER_STRIDE), :]
            for r1 in range(GATHER_STRIDE):
                for k in range(outer):
                    rows = part_ref[buf, r1, pl.ds(k, tm // dilation, stride=outer), :]
                    dst[out_col, r1 + GATHER_STRIDE * k] = rows.astype(dst.dtype)


def _in_proj_a(x, g_all, layer_g, w, casts, *, tm, tn):
    m, d = x.shape
    n = w.shape[1]
    batch = m // SEQ
    tiles_per_seq = SEQ // tm
    assert m % tm == 0 and SEQ % tm == 0 and n == A_IN and n % tn == 0 and tn % HEAD_DIM == 0
    out_specs, out_shapes = [], []
    for _, dilation in A_PATTERNS:
        assert tm % (dilation * BF16_ROWS) == 0
        out_specs.append(pl.BlockSpec(
            (3 * HEADS_PER_GROUP, None, dilation, tm // dilation, HEAD_DIM),
            lambda i: (0, i // tiles_per_seq, 0, i % tiles_per_seq, 0)))
        out_shapes.append(jax.ShapeDtypeStruct(
            (3 * HEADS_PER_GROUP, batch, dilation, SEQ // dilation, HEAD_DIM), jnp.bfloat16))
    out_specs.append(pl.BlockSpec((MEM_HEADS, tm, HEAD_DIM), lambda i: (0, i, 0)))
    out_shapes.append(jax.ShapeDtypeStruct((MEM_HEADS, m, HEAD_DIM), jnp.bfloat16))
    cast_in_specs, cast_out_specs, cast_out_shapes = _cast_specs(casts, m // tm, lambda i: i)
    outs = pl.pallas_call(
        functools.partial(_in_proj_a_kernel, tn=tn, n_casts=len(casts)),
        grid=(m // tm,),
        in_specs=[
            pl.BlockSpec((tm, d), lambda i: (i, 0)),
            _layer_spec((1, d), lambda i: (0, 0), layer_g),
            _resident(w.shape),
        ] + cast_in_specs,
        out_specs=out_specs + cast_out_specs,
        out_shape=out_shapes + cast_out_shapes,
        scratch_shapes=[pltpu.VMEM((tm, d), jnp.bfloat16),
                        pltpu.VMEM((N_SLABS, tm, HEAD_DIM), jnp.float32),
                        pltpu.VMEM((N_SLABS, GATHER_STRIDE, tm // GATHER_STRIDE, HEAD_DIM),
                                   jnp.float32)],
        compiler_params=_params(1),
        name="in_proj_a",
    )(x, g_all, w, *[c[0] for c in casts])
    return outs[:A_GROUPS], outs[A_GROUPS], outs[A_GROUPS + 1:]


def _mem_kv_kernel(x_ref, g_ref, w_ref, o_ref, h_ref):
    _rms_to(h_ref, x_ref, g_ref[...])
    w = w_ref[...].astype(jnp.bfloat16)
    o_ref[...] = jnp.dot(h_ref[...], w, preferred_element_type=jnp.float32).astype(o_ref.dtype)


def _mem_kv(mems, g_all, w_all):
    m, d = mems.shape
    layers, _, n = w_all.shape
    return pl.pallas_call(
        _mem_kv_kernel,
        grid=(layers,),
        in_specs=[
            pl.BlockSpec((m, d), lambda l: (0, 0)),
            pl.BlockSpec((None, 1, d), lambda l: (l, 0, 0)),
            pl.BlockSpec((None, d, n), lambda l: (l, 0, 0)),
        ],
        out_specs=pl.BlockSpec((None, m, n), lambda l: (l, 0, 0)),
        out_shape=jax.ShapeDtypeStruct((layers, m, n), jnp.bfloat16),
        scratch_shapes=[pltpu.VMEM((m, d), jnp.bfloat16)],
        compiler_params=_params(1),
        name="mem_kv",
    )(mems, g_all, w_all)


def _group_geometry(grp):
    window, dilation = A_PATTERNS[grp]
    n_side = (window // 2) // dilation
    length = SEQ // dilation
    kw = min(length, QBLK + 2 * n_side)
    return dilation, n_side, length, kw


def _key_start(qi, n_side, length, kw):
    return min(max(qi * QBLK - n_side, 0), length - kw)


def _dilated_attn_kernel(slopes_ref, *refs, n_casts):
    qkv_refs, refs = refs[:3 * A_GROUPS], refs[3 * A_GROUPS:]
    cast_in, refs = refs[:n_casts], refs[n_casts:]
    tok_ref, refs = refs[0], refs[1:]
    cast_out, refs = refs[:n_casts], refs[n_casts:]
    o_scr, lse_scr, bias_scr = refs[:3]
    stage = refs[3:]
    head = pl.program_id(1)
    scale = HEAD_DIM ** -0.5
    _run_casts(cast_in, cast_out)

    for grp in range(A_GROUPS):
        dilation, n_side, length, kw = _group_geometry(grp)
        nblk = length // QBLK
        q_ref, k_ref, v_ref = qkv_refs[3 * grp:3 * grp + 3]
        vs_ref = stage[grp]
        slope = slopes_ref[grp, head]

        vs_ref[:, :, 0:HEAD_DIM] = v_ref[...]
        vs_ref[:, :, HEAD_DIM:2 * HEAD_DIM] = jnp.ones((dilation, length, HEAD_DIM), jnp.bfloat16)

        offsets = sorted({qi * QBLK - _key_start(qi, n_side, length, kw) for qi in range(nblk)})
        for t, off in enumerate(offsets):
            rel = (lax.broadcasted_iota(jnp.int32, (QBLK, kw), 1)
                   - lax.broadcasted_iota(jnp.int32, (QBLK, kw), 0) - off)
            dist = jnp.abs(rel)
            alibi = (-slope) * (dist * dilation).astype(jnp.float32)
            bias_scr[grp, t, :, 0:kw] = jnp.where(dist <= n_side, alibi, NEG)

        for r in range(dilation):
            for qi in range(nblk):
                k0 = _key_start(qi, n_side, length, kw)
                t = offsets.index(qi * QBLK - k0)
                q = q_ref[r, qi * QBLK:(qi + 1) * QBLK, :]
                k = k_ref[r, k0:k0 + kw, :]
                v1 = vs_ref[r, k0:k0 + kw, :]
                bias = bias_scr[grp, t, :, 0:kw]
                s = lax.dot_general(q, k, (((1,), (1,)), ((), ())),
                                    preferred_element_type=jnp.float32) * scale
                s = jnp.where(bias > 0.5 * NEG, s + bias, NEG)
                m = jnp.max(s, axis=-1, keepdims=True)
                p = jnp.exp(s - m).astype(jnp.bfloat16)
                ol = jnp.dot(p, v1, preferred_element_type=jnp.float32)
                l = ol[:, HEAD_DIM:]
                start = qi * QBLK * dilation + r
                dst = pl.ds(start, QBLK, stride=dilation) if dilation > 1 else pl.ds(start, QBLK)
                o_scr[grp, dst, :] = ol[:, :HEAD_DIM] / l
                lse_scr[grp, dst, :] = m + jnp.log(l)

    rows_per_step = 256
    for c in range(SEQ // rows_per_step):
        rows = slice(c * rows_per_step, (c + 1) * rows_per_step)
        lses = [lse_scr[grp, rows, :] for grp in range(A_GROUPS)]
        mx = functools.reduce(jnp.maximum, lses)
        es = [jnp.exp(l - mx) for l in lses]
        den = functools.reduce(lambda a, b: a + b, es)
        tok = sum((e / den) * o_scr[grp, rows, :] for grp, e in enumerate(es))
        tok_ref[rows, :] = tok.astype(tok_ref.dtype)


def _dilated_attention(qkv, casts, *, batch):
    slopes = jnp.asarray(_alibi_slopes().reshape(A_GROUPS, HEADS_PER_GROUP))
    in_specs, operands, stage = [], [], []
    for grp in range(A_GROUPS):
        dilation, _, length, _ = _group_geometry(grp)
        for which in range(3):
            in_specs.append(pl.BlockSpec(
                (None, None, dilation, length, HEAD_DIM),
                lambda b, h, sl, which=which: (which * HEADS_PER_GROUP + h, b, 0, 0, 0)))
            operands.append(qkv[grp])
        stage.append(pltpu.VMEM((dilation, length, 2 * HEAD_DIM), jnp.bfloat16))
    max_kw = max(_group_geometry(grp)[3] for grp in range(A_GROUPS))
    cast_in_specs, cast_out_specs, cast_out_shapes = _cast_specs(
        casts, batch * HEADS_PER_GROUP, lambda b, h, sl: b * HEADS_PER_GROUP + h)
    outs = pl.pallas_call(
        functools.partial(_dilated_attn_kernel, n_casts=len(casts)),
        grid_spec=pltpu.PrefetchScalarGridSpec(
            num_scalar_prefetch=1,
            grid=(batch, HEADS_PER_GROUP),
            in_specs=in_specs + cast_in_specs,
            out_specs=[pl.BlockSpec((SEQ, HEAD_DIM), lambda b, h, sl: (b, h))] + cast_out_specs,
            scratch_shapes=[
                pltpu.VMEM((A_GROUPS, SEQ, HEAD_DIM), jnp.float32),
                pltpu.VMEM((A_GROUPS, SEQ, HEAD_DIM), jnp.float32),
                pltpu.VMEM((A_GROUPS, 3, QBLK, max_kw), jnp.float32),
            ] + stage,
        ),
        out_shape=[jax.ShapeDtypeStruct((batch * SEQ, GROUP_W), jnp.bfloat16)] + cast_out_shapes,
        compiler_params=_params(2),
        name="dilated_attn",
    )(slopes, *operands, *[c[0] for c in casts])
    return outs[0], outs[1:]


def _mem_attention(q_heads, kv_ref):
    scale = HEAD_DIM ** -0.5
    outs = []
    for h, q in enumerate(q_heads):
        k = kv_ref[:, h * HEAD_DIM:(h + 1) * HEAD_DIM]
        v = kv_ref[:, MEM_W + h * HEAD_DIM:MEM_W + (h + 1) * HEAD_DIM]
        s = lax.dot_general(q, k, (((1,), (1,)), ((), ())),
                            preferred_element_type=jnp.float32) * scale
        m = jnp.max(s, axis=-1, keepdims=True)
        p = jnp.exp(s - m)
        l = jnp.sum(p, axis=-1, keepdims=True)
        o = jnp.dot(p.astype(jnp.bfloat16), v, preferred_element_type=jnp.float32)
        outs.append(o / l)
    return jnp.concatenate(outs, axis=-1)


def _mix_out_a_kernel(x_ref, tok_ref, qm_ref, kv_ref, w_ref, out_ref):
    q_heads = [qm_ref[h] for h in range(MEM_HEADS)]
    mem_out = _mem_attention(q_heads, kv_ref).astype(jnp.bfloat16)
    cat = jnp.concatenate([tok_ref[...], mem_out], axis=-1)
    out_ref[...] = x_ref[...] + jnp.dot(cat, w_ref[...], preferred_element_type=jnp.float32)


def _kv_spec(layer, batch, tiles_per_seq):
    return pl.BlockSpec((None, MEM_LEN, 2 * MEM_W),
                        lambda i: (layer * batch + i // tiles_per_seq, 0, 0))


def _mix_out_a(x, tok, qm, kv, w_out, layer, *, tm):
    m, d = x.shape
    tiles_per_seq = SEQ // tm
    batch = m // SEQ
    row = lambda i: (i, 0)
    return pl.pallas_call(
        _mix_out_a_kernel,
        grid=(m // tm,),
        in_specs=[
            pl.BlockSpec((tm, d), row),
            pl.BlockSpec((tm, GROUP_W), row),
            pl.BlockSpec((MEM_HEADS, tm, HEAD_DIM), lambda i: (0, i, 0)),
            _kv_spec(layer, batch, tiles_per_seq),
            pl.BlockSpec(w_out.shape, lambda i: (0, 0)),
        ],
        out_specs=pl.BlockSpec((tm, d), row),
        out_shape=jax.ShapeDtypeStruct((m, d), jnp.float32),
        compiler_params=_params(1),
        name="mix_out_a",
    )(x, tok, qm, kv, w_out)


def _mixer_b_kernel(x_ref, g_ref, win_ref, vg_ref, ws_ref, sb_ref, kv_ref, wout_ref, out_ref,
                    h_ref, uvq_ref, tok_ref, *, tn):
    tm = x_ref.shape[0]
    _rms_to(h_ref, x_ref, g_ref[...])
    for c in range(B_IN // tn):
        acc = jnp.dot(h_ref[...], win_ref[:, c * tn:(c + 1) * tn],
                      preferred_element_type=jnp.float32)
        n_gelu = min(max(2 * B_W - c * tn, 0), tn)
        if n_gelu:
            uvq_ref[:, c * tn:c * tn + n_gelu] = _gelu(acc[:, :n_gelu])
        if n_gelu < tn:
            uvq_ref[:, c * tn + n_gelu:(c + 1) * tn] = acc[:, n_gelu:]

    vn = _rms(uvq_ref[:, B_W:2 * B_W], vg_ref[...]).astype(jnp.bfloat16)
    n_chunks = tm // CHUNK
    for g in range(B_GROUPS):
        cols = slice(g * B_GROUP_W, (g + 1) * B_GROUP_W)
        v_g = jnp.concatenate([vn[c * CHUNK:(c + 1) * CHUNK, cols] for c in range(n_chunks)],
                              axis=1)
        mixed = jnp.dot(ws_ref[g], v_g, preferred_element_type=jnp.float32)
        mixed = mixed + sb_ref[:, g:g + 1]
        for c in range(n_chunks):
            rows = slice(c * CHUNK, (c + 1) * CHUNK)
            tok_ref[rows, cols] = (uvq_ref[rows, cols]
                                   * mixed[:, c * B_GROUP_W:(c + 1) * B_GROUP_W]
                                   ).astype(jnp.bfloat16)
    qm = uvq_ref[:, 2 * B_W:].astype(jnp.bfloat16)
    q_heads = [qm[:, hd * HEAD_DIM:(hd + 1) * HEAD_DIM] for hd in range(MEM_HEADS)]
    mem_out = _mem_attention(q_heads, kv_ref).astype(jnp.bfloat16)
    cat = jnp.concatenate([tok_ref[...], mem_out], axis=-1)
    out_ref[...] = x_ref[...] + jnp.dot(cat, wout_ref[...], preferred_element_type=jnp.float32)


def _mixer_b(x, g_all, w_in, kv, v_norm_g_all, w_s_all, s_bias_t_all, w_out, layer, layer_b, *,
             tm, tn):
    m, d = x.shape
    assert m % tm == 0 and SEQ % tm == 0 and tm % CHUNK == 0 and B_IN % tn == 0
    tiles_per_seq = SEQ // tm
    batch = m // SEQ
    row = lambda i: (i, 0)
    const2 = lambda i: (0, 0)
    return pl.pallas_call(
        functools.partial(_mixer_b_kernel, tn=tn),
        grid=(m // tm,),
        in_specs=[
            pl.BlockSpec((tm, d), row),
            _layer_spec((1, d), const2, layer),
            _resident(w_in.shape),
            _layer_spec((1, B_W), const2, layer_b),
            _layer_spec(w_s_all.shape[1:], lambda i: (0, 0, 0), layer_b),
            _layer_spec(s_bias_t_all.shape[1:], const2, layer_b),
            _kv_spec(layer, batch, tiles_per_seq),
            _resident(w_out.shape),
        ],
        out_specs=pl.BlockSpec((tm, d), row),
        out_shape=jax.ShapeDtypeStruct((m, d), jnp.float32),
        scratch_shapes=[
            pltpu.VMEM((tm, d), jnp.bfloat16),
            pltpu.VMEM((tm, B_IN), jnp.float32),
            pltpu.VMEM((tm, B_W), jnp.bfloat16),
        ],
        compiler_params=_params(1),
        name="mixer_b",
    )(x, g_all, w_in, v_norm_g_all, w_s_all, s_bias_t_all, kv, w_out)


HALO = BF16_ROWS
FFN_TM = 1024
FFN_TF = 512
MXU_N = 256


N_FFN_IN = 9
CONV_TAPS = 3


def _conv_ffn_kernel(*refs, n_casts, tiles_per_seq, final_norm):
    (x_hbm, xp_ref, xn_ref, g_ref, wg_ref, wv_ref, cp_ref, wd_ref,
     fg_ref) = refs[:N_FFN_IN]
    cast_in = refs[N_FFN_IN:N_FFN_IN + n_casts]
    out_ref = refs[N_FFN_IN + n_casts]
    cast_out = refs[N_FFN_IN + n_casts + 1:N_FFN_IN + 2 * n_casts + 1]
    h_ref, ag_ref, av_ref, x_ref, x_sem = refs[N_FFN_IN + 2 * n_casts + 1:]
    i = pl.program_id(0)
    j = pl.program_id(1)
    n_tiles = pl.num_programs(0)
    nj = pl.num_programs(1)
    tm = x_ref.shape[0]

    def x_copy(tile):
        return pltpu.make_async_copy(x_hbm.at[pl.ds(tile * tm, tm), :], x_ref, x_sem)

    @pl.when(jnp.logical_and(i == 0, j == 0))
    def _():
        x_copy(0).start()

    @pl.when(j == 0)
    def _():
        x_copy(i).wait()
        g = g_ref[...]
        _rms_to(h_ref, x_ref, g, copy_ref=out_ref)
        first = (i % tiles_per_seq) == 0
        last = (i % tiles_per_seq) == tiles_per_seq - 1
        r = lax.broadcasted_iota(jnp.int32, (HALO, 1), 0)
        take_next = jnp.logical_and(r == 0, jnp.logical_not(last))
        take_prev = jnp.logical_and(r == HALO - 1, jnp.logical_not(first))
        halo = jnp.where(take_next, _rms(xn_ref[...], g),
                         jnp.where(take_prev, _rms(xp_ref[...], g), 0.0))
        h_ref[tm:tm + HALO, :] = halo.astype(jnp.bfloat16)

    @pl.when(jnp.logical_and(j == 1, i + 1 < n_tiles))
    def _():
        x_copy(i + 1).start()

    _run_casts(cast_in, cast_out)

    def up(w_ref, a_ref):
        a = jnp.dot(h_ref[...], w_ref[...], preferred_element_type=jnp.float32)
        a_ref[HALO:HALO + tm, :] = a[0:tm]
        a_ref[0:HALO, :] = a[tm:tm + HALO]
        a_ref[HALO + tm:2 * HALO + tm, :] = a[tm:tm + HALO]

    def conv(a_ref, half, cols):
        cp = cp_ref[half]
        return (a_ref[HALO - 1:HALO - 1 + tm, cols] * cp[0:1, cols]
                + a_ref[HALO:HALO + tm, cols] * cp[1:2, cols]
                + a_ref[HALO + 1:HALO + 1 + tm, cols] * cp[2:3, cols]
                + cp[CONV_TAPS:CONV_TAPS + 1, cols])

    up(wg_ref, ag_ref)
    up(wv_ref, av_ref)
    tf = wg_ref.shape[1]
    for c in range(tf // MXU_N):
        cols = slice(c * MXU_N, (c + 1) * MXU_N)
        act = (_gelu(conv(ag_ref, 0, cols)) * conv(av_ref, 1, cols)).astype(jnp.bfloat16)
        out_ref[...] += jnp.dot(act, wd_ref[cols, :], preferred_element_type=jnp.float32)

    if final_norm:
        @pl.when(j == nj - 1)
        def _():
            _rms_to(out_ref, out_ref, fg_ref[...])


def _conv_ffn(x, g_all, w_up, w_down, cp_all, final_g, layer, casts, *, tm, tf, final_norm):
    m, d = x.shape
    nf = FF // tf
    assert m % tm == 0 and FF % tf == 0 and SEQ % tm == 0 and tm % HALO == 0
    assert nf >= 2
    n_tiles = m // tm
    n_steps = n_tiles * nf
    tiles_per_seq = SEQ // tm
    hb = tm // HALO
    n_hblocks = m // HALO

    cast_in_specs, cast_out_specs, cast_out_shapes = _cast_specs(
        casts, n_steps, lambda i, j: i * nf + j)

    outs = pl.pallas_call(
        functools.partial(_conv_ffn_kernel, n_casts=len(casts),
                          tiles_per_seq=tiles_per_seq, final_norm=final_norm),
        grid=(n_tiles, nf),
        in_specs=[
            pl.BlockSpec(memory_space=pl.ANY),
            pl.BlockSpec((HALO, d), lambda i, j: (jnp.maximum(i * hb - 1, 0), 0)),
            pl.BlockSpec((HALO, d), lambda i, j: (jnp.minimum((i + 1) * hb, n_hblocks - 1), 0)),
            _layer_spec((1, d), lambda i, j: (0, 0), layer),
            pl.BlockSpec((d, tf), lambda i, j: (0, j)),
            pl.BlockSpec((d, tf), lambda i, j: (0, nf + j)),
            _layer_spec((2, CONV_TAPS + 1, tf), lambda i, j: (0, 0, j), layer),
            pl.BlockSpec((tf, d), lambda i, j: (j, 0)),
            pl.BlockSpec((1, d), lambda i, j: (0, 0)),
        ] + cast_in_specs,
        out_specs=[pl.BlockSpec((tm, d), lambda i, j: (i, 0))] + cast_out_specs,
        out_shape=[jax.ShapeDtypeStruct((m, d), jnp.float32)] + cast_out_shapes,
        scratch_shapes=[
            pltpu.VMEM((tm + HALO, d), jnp.bfloat16),
            pltpu.VMEM((tm + 2 * HALO, tf), jnp.float32),
            pltpu.VMEM((tm + 2 * HALO, tf), jnp.float32),
            pltpu.VMEM((tm, d), jnp.float32),
            pltpu.SemaphoreType.DMA(()),
        ],
        compiler_params=_params(2),
        name="conv_ffn",
    )(x, x, x, g_all, w_up, w_up, cp_all, w_down, final_g, *[c[0] for c in casts])
    return outs[0], outs[1:]


def kernel(x, mem, mix_norm_g, ffn_norm_g, mem_norm_g, w_mem_kv, a_w_in, a_w_out, b_w_in,
           b_v_norm_g, b_w_s, b_s_bias, b_w_out, ffn_w_up, ffn_conv_w, ffn_conv_b, ffn_w_down,
           final_norm_g):
    batch, seq, d = x.shape
    assert (seq, d) == (SEQ, D_MODEL)
    bf = jnp.bfloat16
    xs = x.reshape(batch * seq, d)
    mems = mem.reshape(batch * MEM_LEN, d)

    mix_g = mix_norm_g.reshape(DEPTH, 1, d)
    ffn_g = ffn_norm_g.reshape(DEPTH, 1, d)
    mem_g = mem_norm_g.reshape(DEPTH, 1, d)
    final_g = final_norm_g.reshape(1, d)
    v_norm_g = b_v_norm_g.reshape(-1, 1, B_W)
    s_bias_t = jnp.swapaxes(b_s_bias, 1, 2)
    conv_p = jnp.concatenate([ffn_conv_w.reshape(DEPTH, CONV_TAPS, 2, FF).transpose(0, 2, 1, 3),
                              ffn_conv_b.reshape(DEPTH, 2, 1, FF)], axis=2)
    b_w_s = b_w_s.astype(bf)
    kv = _mem_kv(mems, mem_g, w_mem_kv).reshape(DEPTH * batch, MEM_LEN, 2 * MEM_W)

    def mixer_weights(i):
        return (a_w_in, a_w_out) if i % 2 == 0 else (b_w_in, b_w_out)

    w_in = a_w_in[0].astype(bf)
    w_out = w_up = w_down = None

    for i in range(DEPTH):
        j = i // 2
        if i % 2 == 0:
            first = i == 0
            qkv, qm, cast = _in_proj_a(xs, mix_g, i, w_in,
                                       [(a_w_out, 0), (ffn_w_down, 0)] if first else [],
                                       tm=512, tn=1280)
            if first:
                w_out, w_down = cast
            tok, cast = _dilated_attention(qkv, [(ffn_w_up, 0)] if first else [], batch=batch)
            if first:
                w_up, = cast
            xs = _mix_out_a(xs, tok, qm, kv, w_out, i, tm=1024)
        else:
            xs = _mixer_b(xs, mix_g, w_in, kv, v_norm_g, b_w_s, s_bias_t, w_out, i, j,
                          tm=512, tn=1792)
        casts = []
        if i + 1 < DEPTH:
            nxt_in, nxt_out = mixer_weights(i + 1)
            casts = [(nxt_in, (i + 1) // 2), (nxt_out, (i + 1) // 2), (ffn_w_up, i + 1),
                     (ffn_w_down, i + 1)]
        xs, nxt = _conv_ffn(xs, ffn_g, w_up, w_down, conv_p, final_g, i, casts,
                            tm=FFN_TM, tf=FFN_TF, final_norm=(i == DEPTH - 1))
        if nxt:
            w_in, w_out, w_up, w_down = nxt
    return xs.reshape(batch, seq, d)
```

```python
import functools

import numpy as np
import jax
import jax.numpy as jnp
from jax import lax
from jax.experimental import pallas as pl
from jax.experimental.pallas import tpu as pltpu

D_MODEL = 2048
SEQ = 2048
DEPTH = 4
EPS = 1e-6
NEG = -1e30

HEAD_DIM = 128
HEADS_PER_GROUP = 4
A_PATTERNS = ((128, 1), (512, 4), (2048, 16))
A_GROUPS = len(A_PATTERNS)
A_HEADS = HEADS_PER_GROUP * A_GROUPS
A_QKV_W = A_HEADS * HEAD_DIM
GROUP_W = HEADS_PER_GROUP * HEAD_DIM
QBLK = 128

CHUNK = 128
B_GROUPS = 12
B_GROUP_W = 128
B_W = B_GROUPS * B_GROUP_W

MEM_LEN = 256
MEM_HEADS = 4
MEM_W = MEM_HEADS * HEAD_DIM

A_IN = 3 * A_QKV_W + MEM_W
B_IN = 2 * B_W + MEM_W
FF = 5632

VMEM_LIMIT_BYTES = 62 * 1024 * 1024

_SQRT_HALF = 0.7071067811865476


def _params(n_axes):
    return pltpu.CompilerParams(
        dimension_semantics=("arbitrary",) * n_axes,
        vmem_limit_bytes=VMEM_LIMIT_BYTES,
    )


def _rms(x, g):
    y = x * lax.rsqrt(jnp.mean(x * x, axis=-1, keepdims=True) + EPS)
    return y * g


NORM_ROWS = 16


def _rms_to(dst_ref, x_ref, g, copy_ref=None):
    n = x_ref.shape[0]
    for r0 in range(0, n, NORM_ROWS):
        rows = slice(r0, r0 + NORM_ROWS)
        x = x_ref[rows, :]
        dst_ref[rows, :] = _rms(x, g).astype(dst_ref.dtype)
        if copy_ref is not None:
            copy_ref[rows, :] = x


def _gelu(x):
    return 0.5 * x * (1.0 + lax.erf(x * _SQRT_HALF))


def _alibi_slopes():
    return (2.0 ** (-8.0 * (np.arange(A_HEADS) + 1) / A_HEADS)).astype(np.float32)


def _layer_spec(block, index_map, layer):
    return pl.BlockSpec((None,) + block, lambda *g: (layer,) + index_map(*g))


def _resident(shape):
    return pl.BlockSpec(shape, lambda *g: (0,) * len(shape), pipeline_mode=pl.Buffered(1))


BF16_ROWS = 16


def _cast_specs(casts, n_steps, step_of):
    in_specs, out_specs, out_shapes = [], [], []
    for src, src_layer in casts:
        _, rows, width = src.shape
        block_rows = BF16_ROWS * pl.cdiv(rows, BF16_ROWS * n_steps)
        n_blocks = rows // block_rows
        assert rows % block_rows == 0 and n_blocks <= n_steps

        def imap(*g, n_blocks=n_blocks):
            return (jnp.minimum(step_of(*g), n_blocks - 1), 0)

        in_specs.append(_layer_spec((block_rows, width), imap, src_layer))
        out_specs.append(pl.BlockSpec((block_rows, width), imap))
        out_shapes.append(jax.ShapeDtypeStruct((rows, width), jnp.bfloat16))
    return in_specs, out_specs, out_shapes


def _run_casts(cast_in, cast_out):
    for src, dst in zip(cast_in, cast_out):
        dst[...] = src[...].astype(jnp.bfloat16)


N_SLABS = 4
GATHER_STRIDE = 4


def _in_proj_a_kernel(x_ref, g_ref, w_ref, *refs, tn, n_casts):
    cast_in, refs = refs[:n_casts], refs[n_casts:]
    group_refs = refs[:A_GROUPS]
    qm_ref = refs[A_GROUPS]
    cast_out = refs[A_GROUPS + 1:A_GROUPS + 1 + n_casts]
    h_ref, slab_ref, part_ref = refs[A_GROUPS + 1 + n_casts:]
    tm = x_ref.shape[0]
    _run_casts(cast_in, cast_out)
    _rms_to(h_ref, x_ref, g_ref[...])
    n_slab = 0
    for c in reversed(range(w_ref.shape[1] // tn)):
        acc = jnp.dot(h_ref[...], w_ref[:, c * tn:(c + 1) * tn],
                      preferred_element_type=jnp.float32)
        for cc in range(tn // HEAD_DIM):
            col = c * (tn // HEAD_DIM) + cc
            slab = acc[:, cc * HEAD_DIM:(cc + 1) * HEAD_DIM]
            which, rest = divmod(col, A_HEADS)
            if which == 3:
                qm_ref[rest] = slab.astype(qm_ref.dtype)
                continue
            grp, head = divmod(rest, HEADS_PER_GROUP)
            dst = group_refs[grp]
            dilation = A_PATTERNS[grp][1]
            if dilation == 1:
                dst[which * HEADS_PER_GROUP + head, 0] = slab.astype(dst.dtype)
                continue
            buf = n_slab % N_SLABS
            n_slab += 1
            slab_ref[buf] = slab
            out_col = which * HEADS_PER_GROUP + head
            if dilation <= GATHER_STRIDE:
                for r in range(dilation):
                    rows = slab_ref[buf, pl.ds(r, tm // dilation, stride=dilation), :]
                    dst[out_col, r] = rows.astype(dst.dtype)
                continue
            outer = dilation // GATHER_STRIDE
            for r1 in range(GATHER_STRIDE):
                part_ref[buf, r1] = slab_ref[buf, pl.ds(r1, tm // GATHER_STRIDE,
                                                        stride=GATHER_STRIDE), :]
            for r1 in range(GATHER_STRIDE):
                for k in range(outer):
                    rows = part_ref[buf, r1, pl.ds(k, tm // dilation, stride=outer), :]
                    dst[out_col, r1 + GATHER_STRIDE * k] = rows.astype(dst.dtype)


def _in_proj_a(x, g_all, layer_g, w, casts, *, tm, tn):
    m, d = x.shape
    n = w.shape[1]
    batch = m // SEQ
    tiles_per_seq = SEQ // tm
    assert m % tm == 0 and SEQ % tm == 0 and n == A_IN and n % tn == 0 and tn % HEAD_DIM == 0
    out_specs, out_shapes = [], []
    for _, dilation in A_PATTERNS:
        assert tm % (dilation * BF16_ROWS) == 0
        out_specs.append(pl.BlockSpec(
            (3 * HEADS_PER_GROUP, None, dilation, tm // dilation, HEAD_DIM),
            lambda i: (0, i // tiles_per_seq, 0, i % tiles_per_seq, 0)))
        out_shapes.append(jax.ShapeDtypeStruct(
            (3 * HEADS_PER_GROUP, batch, dilation, SEQ // dilation, HEAD_DIM), jnp.bfloat16))
    out_specs.append(pl.BlockSpec((MEM_HEADS, tm, HEAD_DIM), lambda i: (0, i, 0)))
    out_shapes.append(jax.ShapeDtypeStruct((MEM_HEADS, m, HEAD_DIM), jnp.bfloat16))
    cast_in_specs, cast_out_specs, cast_out_shapes = _cast_specs(casts, m // tm, lambda i: i)
    outs = pl.pallas_call(
        functools.partial(_in_proj_a_kernel, tn=tn, n_casts=len(casts)),
        grid=(m // tm,),
        in_specs=[
            pl.BlockSpec((tm, d), lambda i: (i, 0)),
            _layer_spec((1, d), lambda i: (0, 0), layer_g),
            _resident(w.shape),
        ] + cast_in_specs,
        out_specs=out_specs + cast_out_specs,
        out_shape=out_shapes + cast_out_shapes,
        scratch_shapes=[pltpu.VMEM((tm, d), jnp.bfloat16),
                        pltpu.VMEM((N_SLABS, tm, HEAD_DIM), jnp.float32),
                        pltpu.VMEM((N_SLABS, GATHER_STRIDE, tm // GATHER_STRIDE, HEAD_DIM),
                                   jnp.float32)],
        compiler_params=_params(1),
        name="in_proj_a",
    )(x, g_all, w, *[c[0] for c in casts])
    return outs[:A_GROUPS], outs[A_GROUPS], outs[A_GROUPS + 1:]


def _mem_kv_kernel(x_ref, g_ref, w_ref, o_ref, h_ref):
    _rms_to(h_ref, x_ref, g_ref[...])
    w = w_ref[...].astype(jnp.bfloat16)
    o_ref[...] = jnp.dot(h_ref[...], w, preferred_element_type=jnp.float32).astype(o_ref.dtype)


def _mem_kv(mems, g_all, w_all):
    m, d = mems.shape
    layers, _, n = w_all.shape
    return pl.pallas_call(
        _mem_kv_kernel,
        grid=(layers,),
        in_specs=[
            pl.BlockSpec((m, d), lambda l: (0, 0)),
            pl.BlockSpec((None, 1, d), lambda l: (l, 0, 0)),
            pl.BlockSpec((None, d, n), lambda l: (l, 0, 0)),
        ],
        out_specs=pl.BlockSpec((None, m, n), lambda l: (l, 0, 0)),
        out_shape=jax.ShapeDtypeStruct((layers, m, n), jnp.bfloat16),
        scratch_shapes=[pltpu.VMEM((m, d), jnp.bfloat16)],
        compiler_params=_params(1),
        name="mem_kv",
    )(mems, g_all, w_all)


def _group_geometry(grp):
    window, dilation = A_PATTERNS[grp]
    n_side = (window // 2) // dilation
    length = SEQ // dilation
    kw = min(length, QBLK + 2 * n_side)
    return dilation, n_side, length, kw


def _key_start(qi, n_side, length, kw):
    return min(max(qi * QBLK - n_side, 0), length - kw)


def _dilated_attn_kernel(slopes_ref, *refs, n_casts):
    qkv_refs, refs = refs[:3 * A_GROUPS], refs[3 * A_GROUPS:]
    cast_in, refs = refs[:n_casts], refs[n_casts:]
    tok_ref, refs = refs[0], refs[1:]
    cast_out, refs = refs[:n_casts], refs[n_casts:]
    o_scr, lse_scr, bias_scr = refs[:3]
    stage = refs[3:]
    head = pl.program_id(1)
    scale = HEAD_DIM ** -0.5
    _run_casts(cast_in, cast_out)

    for grp in range(A_GROUPS):
        dilation, n_side, length, kw = _group_geometry(grp)
        nblk = length // QBLK
        q_ref, k_ref, v_ref = qkv_refs[3 * grp:3 * grp + 3]
        vs_ref = stage[grp]
        slope = slopes_ref[grp, head]

        vs_ref[:, :, 0:HEAD_DIM] = v_ref[...]
        vs_ref[:, :, HEAD_DIM:2 * HEAD_DIM] = jnp.ones((dilation, length, HEAD_DIM), jnp.bfloat16)

        offsets = sorted({qi * QBLK - _key_start(qi, n_side, length, kw) for qi in range(nblk)})
        for t, off in enumerate(offsets):
            rel = (lax.broadcasted_iota(jnp.int32, (QBLK, kw), 1)
                   - lax.broadcasted_iota(jnp.int32, (QBLK, kw), 0) - off)
            dist = jnp.abs(rel)
            alibi = (-slope) * (dist * dilation).astype(jnp.float32)
            bias_scr[grp, t, :, 0:kw] = jnp.where(dist <= n_side, alibi, NEG)

        for r in range(dilation):
            for qi in range(nblk):
                k0 = _key_start(qi, n_side, length, kw)
                t = offsets.index(qi * QBLK - k0)
                q = q_ref[r, qi * QBLK:(qi + 1) * QBLK, :]
                k = k_ref[r, k0:k0 + kw, :]
                v1 = vs_ref[r, k0:k0 + kw, :]
                bias = bias_scr[grp, t, :, 0:kw]
                s = lax.dot_general(q, k, (((1,), (1,)), ((), ())),
                                    preferred_element_type=jnp.float32) * scale
                s = jnp.where(bias > 0.5 * NEG, s + bias, NEG)
                m = jnp.max(s, axis=-1, keepdims=True)
                p = jnp.exp(s - m).astype(jnp.bfloat16)
                ol = jnp.dot(p, v1, preferred_element_type=jnp.float32)
                l = ol[:, HEAD_DIM:]
                start = qi * QBLK * dilation + r
                dst = pl.ds(start, QBLK, stride=dilation) if dilation > 1 else pl.ds(start, QBLK)
                o_scr[grp, dst, :] = ol[:, :HEAD_DIM] / l
                lse_scr[grp, dst, :] = m + jnp.log(l)

    rows_per_step = 256
    for c in range(SEQ // rows_per_step):
        rows = slice(c * rows_per_step, (c + 1) * rows_per_step)
        lses = [lse_scr[grp, rows, :] for grp in range(A_GROUPS)]
        mx = functools.reduce(jnp.maximum, lses)
        es = [jnp.exp(l - mx) for l in lses]
        den = functools.reduce(lambda a, b: a + b, es)
        tok = sum((e / den) * o_scr[grp, rows, :] for grp, e in enumerate(es))
        tok_ref[rows, :] = tok.astype(tok_ref.dtype)


def _dilated_attention(qkv, casts, *, batch):
    slopes = jnp.asarray(_alibi_slopes().reshape(A_GROUPS, HEADS_PER_GROUP))
    in_specs, operands, stage = [], [], []
    for grp in range(A_GROUPS):
        dilation, _, length, _ = _group_geometry(grp)
        for which in range(3):
            in_specs.append(pl.BlockSpec(
                (None, None, dilation, length, HEAD_DIM),
                lambda b, h, sl, which=which: (which * HEADS_PER_GROUP + h, b, 0, 0, 0)))
            operands.append(qkv[grp])
        stage.append(pltpu.VMEM((dilation, length, 2 * HEAD_DIM), jnp.bfloat16))
    max_kw = max(_group_geometry(grp)[3] for grp in range(A_GROUPS))
    cast_in_specs, cast_out_specs, cast_out_shapes = _cast_specs(
        casts, batch * HEADS_PER_GROUP, lambda b, h, sl: b * HEADS_PER_GROUP + h)
    outs = pl.pallas_call(
        functools.partial(_dilated_attn_kernel, n_casts=len(casts)),
        grid_spec=pltpu.PrefetchScalarGridSpec(
            num_scalar_prefetch=1,
            grid=(batch, HEADS_PER_GROUP),
            in_specs=in_specs + cast_in_specs,
            out_specs=[pl.BlockSpec((SEQ, HEAD_DIM), lambda b, h, sl: (b, h))] + cast_out_specs,
            scratch_shapes=[
                pltpu.VMEM((A_GROUPS, SEQ, HEAD_DIM), jnp.float32),
                pltpu.VMEM((A_GROUPS, SEQ, HEAD_DIM), jnp.float32),
                pltpu.VMEM((A_GROUPS, 3, QBLK, max_kw), jnp.float32),
            ] + stage,
        ),
        out_shape=[jax.ShapeDtypeStruct((batch * SEQ, GROUP_W), jnp.bfloat16)] + cast_out_shapes,
        compiler_params=_params(2),
        name="dilated_attn",
    )(slopes, *operands, *[c[0] for c in casts])
    return outs[0], outs[1:]


def _mem_attention(q_heads, kv_ref):
    scale = HEAD_DIM ** -0.5
    outs = []
    for h, q in enumerate(q_heads):
        k = kv_ref[:, h * HEAD_DIM:(h + 1) * HEAD_DIM]
        v = kv_ref[:, MEM_W + h * HEAD_DIM:MEM_W + (h + 1) * HEAD_DIM]
        s = lax.dot_general(q, k, (((1,), (1,)), ((), ())),
                            preferred_element_type=jnp.float32) * scale
        m = jnp.max(s, axis=-1, keepdims=True)
        p = jnp.exp(s - m)
        l = jnp.sum(p, axis=-1, keepdims=True)
        o = jnp.dot(p.astype(jnp.bfloat16), v, preferred_element_type=jnp.float32)
        outs.append(o / l)
    return jnp.concatenate(outs, axis=-1)


def _mix_out_a_kernel(x_ref, tok_ref, qm_ref, kv_ref, w_ref, out_ref):
    q_heads = [qm_ref[h] for h in range(MEM_HEADS)]
    mem_out = _mem_attention(q_heads, kv_ref).astype(jnp.bfloat16)
    cat = jnp.concatenate([tok_ref[...], mem_out], axis=-1)
    out_ref[...] = x_ref[...] + jnp.dot(cat, w_ref[...], preferred_element_type=jnp.float32)


def _kv_spec(layer, batch, tiles_per_seq):
    return pl.BlockSpec((None, MEM_LEN, 2 * MEM_W),
                        lambda i: (layer * batch + i // tiles_per_seq, 0, 0))


def _mix_out_a(x, tok, qm, kv, w_out, layer, *, tm):
    m, d = x.shape
    tiles_per_seq = SEQ // tm
    batch = m // SEQ
    row = lambda i: (i, 0)
    return pl.pallas_call(
        _mix_out_a_kernel,
        grid=(m // tm,),
        in_specs=[
            pl.BlockSpec((tm, d), row),
            pl.BlockSpec((tm, GROUP_W), row),
            pl.BlockSpec((MEM_HEADS, tm, HEAD_DIM), lambda i: (0, i, 0)),
            _kv_spec(layer, batch, tiles_per_seq),
            pl.BlockSpec(w_out.shape, lambda i: (0, 0)),
        ],
        out_specs=pl.BlockSpec((tm, d), row),
        out_shape=jax.ShapeDtypeStruct((m, d), jnp.float32),
        compiler_params=_params(1),
        name="mix_out_a",
    )(x, tok, qm, kv, w_out)


def _mixer_b_kernel(x_ref, g_ref, win_ref, vg_ref, ws_ref, sb_ref, kv_ref, wout_ref, out_ref,
                    h_ref, uvq_ref, tok_ref, *, tn):
    tm = x_ref.shape[0]
    _rms_to(h_ref, x_ref, g_ref[...])
    for c in range(B_IN // tn):
        acc = jnp.dot(h_ref[...], win_ref[:, c * tn:(c + 1) * tn],
                      preferred_element_type=jnp.float32)
        n_gelu = min(max(2 * B_W - c * tn, 0), tn)
        if n_gelu:
            uvq_ref[:, c * tn:c * tn + n_gelu] = _gelu(acc[:, :n_gelu])
        if n_gelu < tn:
            uvq_ref[:, c * tn + n_gelu:(c + 1) * tn] = acc[:, n_gelu:]

    vn = _rms(uvq_ref[:, B_W:2 * B_W], vg_ref[...]).astype(jnp.bfloat16)
    n_chunks = tm // CHUNK
    for g in range(B_GROUPS):
        cols = slice(g * B_GROUP_W, (g + 1) * B_GROUP_W)
        v_g = jnp.concatenate([vn[c * CHUNK:(c + 1) * CHUNK, cols] for c in range(n_chunks)],
                              axis=1)
        mixed = jnp.dot(ws_ref[g], v_g, preferred_element_type=jnp.float32)
        mixed = mixed + sb_ref[:, g:g + 1]
        for c in range(n_chunks):
            rows = slice(c * CHUNK, (c + 1) * CHUNK)
            tok_ref[rows, cols] = (uvq_ref[rows, cols]
                                   * mixed[:, c * B_GROUP_W:(c + 1) * B_GROUP_W]
                                   ).astype(jnp.bfloat16)
    qm = uvq_ref[:, 2 * B_W:].astype(jnp.bfloat16)
    q_heads = [qm[:, hd * HEAD_DIM:(hd + 1) * HEAD_DIM] for hd in range(MEM_HEADS)]
    mem_out = _mem_attention(q_heads, kv_ref).astype(jnp.bfloat16)
    cat = jnp.concatenate([tok_ref[...], mem_out], axis=-1)
    out_ref[...] = x_ref[...] + jnp.dot(cat, wout_ref[...], preferred_element_type=jnp.float32)


def _mixer_b(x, g_all, w_in, kv, v_norm_g_all, w_s_all, s_bias_t_all, w_out, layer, layer_b, *,
             tm, tn):
    m, d = x.shape
    assert m % tm == 0 and SEQ % tm == 0 and tm % CHUNK == 0 and B_IN % tn == 0
    tiles_per_seq = SEQ // tm
    batch = m // SEQ
    row = lambda i: (i, 0)
    const2 = lambda i: (0, 0)
    return pl.pallas_call(
        functools.partial(_mixer_b_kernel, tn=tn),
        grid=(m // tm,),
        in_specs=[
            pl.BlockSpec((tm, d), row),
            _layer_spec((1, d), const2, layer),
            _resident(w_in.shape),
            _layer_spec((1, B_W), const2, layer_b),
            _layer_spec(w_s_all.shape[1:], lambda i: (0, 0, 0), layer_b),
            _layer_spec(s_bias_t_all.shape[1:], const2, layer_b),
            _kv_spec(layer, batch, tiles_per_seq),
            _resident(w_out.shape),
        ],
        out_specs=pl.BlockSpec((tm, d), row),
        out_shape=jax.ShapeDtypeStruct((m, d), jnp.float32),
        scratch_shapes=[
            pltpu.VMEM((tm, d), jnp.bfloat16),
            pltpu.VMEM((tm, B_IN), jnp.float32),
            pltpu.VMEM((tm, B_W), jnp.bfloat16),
        ],
        compiler_params=_params(1),
        name="mixer_b",
    )(x, g_all, w_in, v_norm_g_all, w_s_all, s_bias_t_all, kv, w_out)


HALO = BF16_ROWS
FFN_TM = 1024
FFN_TF = 512
MXU_N = 256


N_FFN_IN = 9
CONV_TAPS = 3


def _conv_ffn_kernel(*refs, n_casts, tiles_per_seq, final_norm):
    (x_hbm, xp_ref, xn_ref, g_ref, wg_ref, wv_ref, cp_ref, wd_ref,
     fg_ref) = refs[:N_FFN_IN]
    cast_in = refs[N_FFN_IN:N_FFN_IN + n_casts]
    out_ref = refs[N_FFN_IN + n_casts]
    cast_out = refs[N_FFN_IN + n_casts + 1:N_FFN_IN + 2 * n_casts + 1]
    h_ref, ag_ref, av_ref, x_ref, x_sem = refs[N_FFN_IN + 2 * n_casts + 1:]
    i = pl.program_id(0)
    j = pl.program_id(1)
    n_tiles = pl.num_programs(0)
    nj = pl.num_programs(1)
    tm = x_ref.shape[0]

    def x_copy(tile):
        return pltpu.make_async_copy(x_hbm.at[pl.ds(tile * tm, tm), :], x_ref, x_sem)

    @pl.when(jnp.logical_and(i == 0, j == 0))
    def _():
        x_copy(0).start()

    @pl.when(j == 0)
    def _():
        x_copy(i).wait()
        g = g_ref[...]
        _rms_to(h_ref, x_ref, g, copy_ref=out_ref)
        first = (i % tiles_per_seq) == 0
        last = (i % tiles_per_seq) == tiles_per_seq - 1
        r = lax.broadcasted_iota(jnp.int32, (HALO, 1), 0)
        take_next = jnp.logical_and(r == 0, jnp.logical_not(last))
        take_prev = jnp.logical_and(r == HALO - 1, jnp.logical_not(first))
        halo = jnp.where(take_next, _rms(xn_ref[...], g),
                         jnp.where(take_prev, _rms(xp_ref[...], g), 0.0))
        h_ref[tm:tm + HALO, :] = halo.astype(jnp.bfloat16)

    @pl.when(jnp.logical_and(j == 1, i + 1 < n_tiles))
    def _():
        x_copy(i + 1).start()

    _run_casts(cast_in, cast_out)

    def up(w_ref, a_ref):
        a = jnp.dot(h_ref[...], w_ref[...], preferred_element_type=jnp.float32)
        a_ref[HALO:HALO + tm, :] = a[0:tm]
        a_ref[0:HALO, :] = a[tm:tm + HALO]
        a_ref[HALO + tm:2 * HALO + tm, :] = a[tm:tm + HALO]

    def conv(a_ref, half, cols):
        cp = cp_ref[half]
        n = tm + 2 * HALO
        a = a_ref[:, cols]
        prev = pltpu.roll(a, 1, axis=0)[HALO:HALO + tm]
        nxt = pltpu.roll(a, n - 1, axis=0)[HALO:HALO + tm]
        return (prev * cp[0:1, cols] + a[HALO:HALO + tm] * cp[1:2, cols]
                + nxt * cp[2:3, cols] + cp[CONV_TAPS:CONV_TAPS + 1, cols])

    up(wg_ref, ag_ref)
    up(wv_ref, av_ref)
    tf = wg_ref.shape[1]
    upd = None
    for c in range(tf // MXU_N):
        cols = slice(c * MXU_N, (c + 1) * MXU_N)
        act = (_gelu(conv(ag_ref, 0, cols)) * conv(av_ref, 1, cols)).astype(jnp.bfloat16)
        part = jnp.dot(act, wd_ref[cols, :], preferred_element_type=jnp.float32)
        upd = part if upd is None else upd + part
    out_ref[...] += upd

    if final_norm:
        @pl.when(j == nj - 1)
        def _():
            _rms_to(out_ref, out_ref, fg_ref[...])


def _conv_ffn(x, g_all, w_up, w_down, cp_all, final_g, layer, casts, *, tm, tf, final_norm):
    m, d = x.shape
    nf = FF // tf
    assert m % tm == 0 and FF % tf == 0 and SEQ % tm == 0 and tm % HALO == 0
    assert nf >= 2
    n_tiles = m // tm
    n_steps = n_tiles * nf
    tiles_per_seq = SEQ // tm
    hb = tm // HALO
    n_hblocks = m // HALO

    cast_in_specs, cast_out_specs, cast_out_shapes = _cast_specs(
        casts, n_steps, lambda i, j: i * nf + j)

    outs = pl.pallas_call(
        functools.partial(_conv_ffn_kernel, n_casts=len(casts),
                          tiles_per_seq=tiles_per_seq, final_norm=final_norm),
        grid=(n_tiles, nf),
        in_specs=[
            pl.BlockSpec(memory_space=pl.ANY),
            pl.BlockSpec((HALO, d), lambda i, j: (jnp.maximum(i * hb - 1, 0), 0)),
            pl.BlockSpec((HALO, d), lambda i, j: (jnp.minimum((i + 1) * hb, n_hblocks - 1), 0)),
            _layer_spec((1, d), lambda i, j: (0, 0), layer),
            pl.BlockSpec((d, tf), lambda i, j: (0, j)),
            pl.BlockSpec((d, tf), lambda i, j: (0, nf + j)),
            _layer_spec((2, CONV_TAPS + 1, tf), lambda i, j: (0, 0, j), layer),
            pl.BlockSpec((tf, d), lambda i, j: (j, 0)),
            pl.BlockSpec((1, d), lambda i, j: (0, 0)),
        ] + cast_in_specs,
        out_specs=[pl.BlockSpec((tm, d), lambda i, j: (i, 0))] + cast_out_specs,
        out_shape=[jax.ShapeDtypeStruct((m, d), jnp.float32)] + cast_out_shapes,
        scratch_shapes=[
            pltpu.VMEM((tm + HALO, d), jnp.bfloat16),
            pltpu.VMEM((tm + 2 * HALO, tf), jnp.float32),
            pltpu.VMEM((tm + 2 * HALO, tf), jnp.float32),
            pltpu.VMEM((tm, d), jnp.float32),
            pltpu.SemaphoreType.DMA(()),
        ],
        compiler_params=_params(2),
        name="conv_ffn",
    )(x, x, x, g_all, w_up, w_up, cp_all, w_down, final_g, *[c[0] for c in casts])
    return outs[0], outs[1:]


def kernel(x, mem, mix_norm_g, ffn_norm_g, mem_norm_g, w_mem_kv, a_w_in, a_w_out, b_w_in,
           b_v_norm_g, b_w_s, b_s_bias, b_w_out, ffn_w_up, ffn_conv_w, ffn_conv_b, ffn_w_down,
           final_norm_g):
    batch, seq, d = x.shape
    assert (seq, d) == (SEQ, D_MODEL)
    bf = jnp.bfloat16
    xs = x.reshape(batch * seq, d)
    mems = mem.reshape(batch * MEM_LEN, d)

    mix_g = mix_norm_g.reshape(DEPTH, 1, d)
    ffn_g = ffn_norm_g.reshape(DEPTH, 1, d)
    mem_g = mem_norm_g.reshape(DEPTH, 1, d)
    final_g = final_norm_g.reshape(1, d)
    v_norm_g = b_v_norm_g.reshape(-1, 1, B_W)
    s_bias_t = jnp.swapaxes(b_s_bias, 1, 2)
    conv_p = jnp.concatenate([ffn_conv_w.reshape(DEPTH, CONV_TAPS, 2, FF).transpose(0, 2, 1, 3),
                              ffn_conv_b.reshape(DEPTH, 2, 1, FF)], axis=2)
    b_w_s = b_w_s.astype(bf)
    kv = _mem_kv(mems, mem_g, w_mem_kv).reshape(DEPTH * batch, MEM_LEN, 2 * MEM_W)

    def mixer_weights(i):
        return (a_w_in, a_w_out) if i % 2 == 0 else (b_w_in, b_w_out)

    w_in = a_w_in[0].astype(bf)
    w_out = w_up = w_down = None

    for i in range(DEPTH):
        j = i // 2
        if i % 2 == 0:
            first = i == 0
            qkv, qm, cast = _in_proj_a(xs, mix_g, i, w_in,
                                       [(a_w_out, 0), (ffn_w_down, 0)] if first else [],
                                       tm=512, tn=1280)
            if first:
                w_out, w_down = cast
            tok, cast = _dilated_attention(qkv, [(ffn_w_up, 0)] if first else [], batch=batch)
            if first:
                w_up, = cast
            xs = _mix_out_a(xs, tok, qm, kv, w_out, i, tm=1024)
        else:
            xs = _mixer_b(xs, mix_g, w_in, kv, v_norm_g, b_w_s, s_bias_t, w_out, i, j,
                          tm=512, tn=1792)
        casts = []
        if i + 1 < DEPTH:
            nxt_in, nxt_out = mixer_weights(i + 1)
            casts = [(nxt_in, (i + 1) // 2), (nxt_out, (i + 1) // 2), (ffn_w_up, i + 1),
                     (ffn_w_down, i + 1)]
        xs, nxt = _conv_ffn(xs, ffn_g, w_up, w_down, conv_p, final_g, i, casts,
                            tm=FFN_TM, tf=FFN_TF, final_norm=(i == DEPTH - 1))
        if nxt:
            w_in, w_out, w_up, w_down = nxt
    return xs.reshape(batch, seq, d)
```
